```python
import math
import jax, jax.numpy as jnp
from jax import lax
import numpy as np

D_MODEL = 1024
BATCH = 16
SEQ = 2048
DEPTH = 4

HEAD_DIM = 64
N_EVEN = (DEPTH + 1) // 2
N_ODD = DEPTH // 2

GLA_W = D_MODEL // 2
GLA_HEADS = GLA_W // HEAD_DIM
GLA_DK = GLA_W // GLA_HEADS
GLA_DV = GLA_W // GLA_HEADS
GLA_RANK = 16
GLA_GATE_NORM = 16.0
GLA_CHUNK = 64
SWA_W = D_MODEL // 2
SWA_Q_HEADS = SWA_W // HEAD_DIM
SWA_KV_HEADS = SWA_Q_HEADS // 4
SWA_KV_W = SWA_KV_HEADS * HEAD_DIM
WINDOW = 128
SWA_BLOCK = 128
S5_W = D_MODEL // 2
S5_GROUP = 16
S5_GROUPS = S5_W // S5_GROUP
S5_STATE = 64
GDN_W = D_MODEL // 2
GDN_HEADS = GDN_W // HEAD_DIM
GDN_DK = GDN_W // GDN_HEADS
GDN_DV = GDN_W // GDN_HEADS
GDN_CONV = 4
GDN_CHUNK = 64
FFN_DIM = 2816
FFN_CONV = 3

EVEN_SPLITS = [GLA_W, GLA_W, GLA_W, GLA_W, GLA_RANK, SWA_W, SWA_KV_W, SWA_KV_W]
ODD_SPLITS = [S5_W, 3 * GDN_W, GDN_W, GDN_HEADS, GDN_HEADS]
EVEN_IN = sum(EVEN_SPLITS)
ODD_IN = sum(ODD_SPLITS)
NORM_EPS = 1e-6

kernel_name = "hybrid_gla_swa_s5_gdn_convffn"


def _split(t, sizes):
    idx = [int(v) for v in np.cumsum(sizes)[:-1]]
    return jnp.split(t, idx, axis=-1)


def rms_norm(x, gain):
    xf = x.astype(jnp.float32)
    y = xf * lax.rsqrt(jnp.mean(xf * xf, axis=-1, keepdims=True) + NORM_EPS)
    return (y * gain.astype(jnp.float32)).astype(x.dtype)


def l2_norm(x):
    xf = x.astype(jnp.float32)
    return xf * lax.rsqrt(jnp.sum(xf * xf, axis=-1, keepdims=True) + NORM_EPS)


def causal_dwconv(x, w):
    k_w, ch = w.shape
    return lax.conv_general_dilated(x, w[:, None, :].astype(x.dtype), window_strides=(1,),
                                    padding=[(k_w - 1, 0)], dimension_numbers=('NWC', 'WIO', 'NWC'),
                                    feature_group_count=ch)


def alibi_slopes(n):
    return jnp.exp2(-8.0 * jnp.arange(1, n + 1, dtype=jnp.float32) / n)


def _chunk(t, c):
    b, l, h, d = t.shape
    return t.reshape(b, l // c, c, h, d).transpose(0, 3, 1, 2, 4)


def _unchunk(t):
    b, h, n, c, d = t.shape
    return t.transpose(0, 2, 3, 1, 4).reshape(b, n * c, h, d)


def gla_attention(q, k, v, log_g):
    c = GLA_CHUNK
    q, k, v, log_g = (_chunk(t.astype(jnp.float32), c) for t in (q, k, v, log_g))
    b_, h_, n_, _, dk = q.shape
    dv = v.shape[-1]
    q = q * dk ** -0.5
    cum = jnp.cumsum(log_g, axis=3)
    cum_last = cum[:, :, :, -1:, :]
    q_pos = q * jnp.exp(cum)
    k_neg = k * jnp.exp(-cum)
    k_dec = k * jnp.exp(cum_last - cum)
    causal = jnp.tril(jnp.ones((c, c), dtype=bool))
    scores = jnp.where(causal, jnp.einsum('bhnid,bhnjd->bhnij', q_pos, k_neg), 0.0)
    o_intra = jnp.einsum('bhnij,bhnjv->bhniv', scores, v)

    def step(state, inp):
        qp, kd, vv, dec = inp
        o = jnp.einsum('bhid,bhdv->bhiv', qp, state)
        state = state * dec[..., None] + jnp.einsum('bhjd,bhjv->bhdv', kd, vv)
        return state, o

    state0 = jnp.zeros((b_, h_, dk, dv), jnp.float32)
    xs = (jnp.moveaxis(q_pos, 2, 0), jnp.moveaxis(k_dec, 2, 0), jnp.moveaxis(v, 2, 0),
          jnp.moveaxis(jnp.exp(cum_last[:, :, :, 0, :]), 2, 0))
    _, o_inter = lax.scan(step, state0, xs)
    return _unchunk(o_intra + jnp.moveaxis(o_inter, 0, 2))


def swa_sink_attention(q, k, v, sinks, slopes):
    b_, l_, hq, hd = q.shape
    hkv = k.shape[2]
    g_ = hq // hkv
    tb = SWA_BLOCK
    nb = l_ // tb
    qb = q.reshape(b_, nb, tb, hkv, g_, hd)
    kb = k.reshape(b_, nb, tb, hkv, hd)
    vb = v.reshape(b_, nb, tb, hkv, hd)
    pad = ((0, 0), (1, 0), (0, 0), (0, 0), (0, 0))
    kk = jnp.concatenate([jnp.pad(kb, pad)[:, :-1], kb], axis=2)
    vv = jnp.concatenate([jnp.pad(vb, pad)[:, :-1], vb], axis=2)
    s = jnp.einsum('bnqhgd,bnkhd->bhgnqk', qb, kk).astype(jnp.float32) * hd ** -0.5
    q_idx = jnp.arange(tb)[:, None] + tb
    k_idx = jnp.arange(2 * tb)[None, :]
    dist = q_idx - k_idx
    valid = (dist >= 0) & (dist < WINDOW)
    valid = valid[None] & ((jnp.arange(nb)[:, None, None] > 0) | (k_idx[None] >= tb))
    sl = slopes.reshape(hkv, g_)[:, :, None, None, None]
    s = s - sl * dist.astype(jnp.float32)
    s = jnp.where(valid, s, -jnp.inf)
    sink = sinks.astype(jnp.float32).reshape(hkv, g_)[:, :, None, None, None]
    m = jnp.maximum(jnp.max(s, axis=-1, keepdims=True), sink)
    p = jnp.exp(s - m)
    p = p / (jnp.sum(p, axis=-1, keepdims=True) + jnp.exp(sink - m))
    o = jnp.einsum('bhgnqk,bnkhd->bnqhgd', p.astype(v.dtype), vv)
    return o.reshape(b_, l_, hq * hd)


def s5_ssm(u, a_re, a_im, log_dt, b_re, b_im, c_re, c_im, d_skip, w_glu, b_glu):
    bsz, seq, width = u.shape
    uf = u.astype(jnp.float32)
    ug = uf.reshape(bsz, seq, S5_GROUPS, S5_GROUP)
    a_re, a_im = a_re.astype(jnp.float32), a_im.astype(jnp.float32)
    dt = jnp.exp(log_dt.astype(jnp.float32))[:, None]
    mag = jnp.exp(dt * a_re)
    ang = dt * a_im
    abar_re, abar_im = mag * jnp.cos(ang), mag * jnp.sin(ang)
    den = a_re * a_re + a_im * a_im
    f_re = ((abar_re - 1.0) * a_re + abar_im * a_im) / den
    f_im = (abar_im * a_re - (abar_re - 1.0) * a_im) / den
    b_re, b_im = b_re.astype(jnp.float32), b_im.astype(jnp.float32)
    bbar_re = f_re[..., None] * b_re - f_im[..., None] * b_im
    bbar_im = f_re[..., None] * b_im + f_im[..., None] * b_re
    bu_re = jnp.einsum('blgc,gpc->blgp', ug, bbar_re)
    bu_im = jnp.einsum('blgc,gpc->blgp', ug, bbar_im)
    at_re = jnp.broadcast_to(abar_re, bu_re.shape)
    at_im = jnp.broadcast_to(abar_im, bu_im.shape)

    def combine(e1, e2):
        a1r, a1i, b1r, b1i = e1
        a2r, a2i, b2r, b2i = e2
        return (a2r * a1r - a2i * a1i, a2r * a1i + a2i * a1r,
                a2r * b1r - a2i * b1i + b2r, a2r * b1i + a2i * b1r + b2i)

    _, _, x_re, x_im = lax.associative_scan(combine, (at_re, at_im, bu_re, bu_im), axis=1)
    y = (jnp.einsum('blgp,gop->blgo', x_re, c_re.astype(jnp.float32))
         - jnp.einsum('blgp,gop->blgo', x_im, c_im.astype(jnp.float32)))
    y = y.reshape(bsz, seq, width) + d_skip.astype(jnp.float32) * uf
    y = jax.nn.gelu(y)
    return y * jax.nn.sigmoid(y @ w_glu.astype(jnp.float32) + b_glu.astype(jnp.float32))


def gated_delta_rule(q, k, v, log_alpha, beta):
    c = GDN_CHUNK
    q, k, v = (_chunk(t.astype(jnp.float32), c) for t in (q, k, v))
    b_, h_, n_, _, dk = q.shape
    dv = v.shape[-1]
    g = jnp.cumsum(_chunk(log_alpha[..., None].astype(jnp.float32), c)[..., 0], axis=-1)
    beta = _chunk(beta[..., None].astype(jnp.float32), c)
    q = q * dk ** -0.5
    idx = jnp.arange(c)
    incl = idx[:, None] >= idx[None, :]
    strict = idx[:, None] > idx[None, :]
    decay_incl = jnp.exp(jnp.where(incl, g[..., :, None] - g[..., None, :], -jnp.inf))
    decay_strict = jnp.where(strict, decay_incl, 0.0)
    lower = beta * jnp.einsum('bhnid,bhnjd->bhnij', k, k) * decay_strict
    rhs = jnp.concatenate([v * beta, k * beta * jnp.exp(g)[..., None]], axis=-1)
    sol = lax.linalg.triangular_solve(jnp.eye(c, dtype=jnp.float32) + lower, rhs,
                                      left_side=True, lower=True, unit_diagonal=True)
    u, w = sol[..., :dv], sol[..., dv:]
    attn = jnp.einsum('bhnid,bhnjd->bhnij', q, k) * decay_incl
    g_last = g[..., -1:]
    q_dec = q * jnp.exp(g)[..., None]
    k_dec = k * jnp.exp(g_last - g)[..., None]

    def step(state, inp):
        u_n, w_n, a_n, qd, kd, dec = inp
        v_new = u_n - jnp.einsum('bhid,bhdv->bhiv', w_n, state)
        o = jnp.einsum('bhid,bhdv->bhiv', qd, state) + jnp.einsum('bhij,bhjv->bhiv', a_n, v_new)
        state = state * dec[..., None, None] + jnp.einsum('bhjd,bhjv->bhdv', kd, v_new)
        return state, o

    state0 = jnp.zeros((b_, h_, dk, dv), jnp.float32)
    xs = tuple(jnp.moveaxis(t, 2, 0) for t in (u, w, attn, q_dec, k_dec, jnp.exp(g_last[..., 0])))
    _, o = lax.scan(step, state0, xs)
    return _unchunk(jnp.moveaxis(o, 0, 2))


def even_mixer(h, w_in, w_gate, b_gate, gla_norm, q_norm, k_norm, sinks, w_out, slopes):
    bsz, seq, _ = h.shape
    heads = lambda t, n: t.reshape(bsz, seq, n, -1)
    proj = h @ w_in
    gq, gk, gv, gr, glr, sq, sk, sv = _split(proj, EVEN_SPLITS)
    log_g = jax.nn.log_sigmoid((glr @ w_gate + b_gate).astype(jnp.float32)) / GLA_GATE_NORM
    o_a = gla_attention(heads(gq, GLA_HEADS), heads(gk, GLA_HEADS), heads(gv, GLA_HEADS),
                        heads(log_g, GLA_HEADS))
    o_a = rms_norm(o_a, gla_norm) * jax.nn.silu(heads(gr, GLA_HEADS).astype(jnp.float32))
    o_a = o_a.reshape(bsz, seq, GLA_W).astype(h.dtype)
    q = rms_norm(heads(sq, SWA_Q_HEADS), q_norm)
    k = rms_norm(heads(sk, SWA_KV_HEADS), k_norm)
    o_b = swa_sink_attention(q, k, heads(sv, SWA_KV_HEADS), sinks, slopes).astype(h.dtype)
    return jnp.concatenate([o_a, o_b], axis=-1) @ w_out


def odd_mixer(h, w_in, a_re, a_im, log_dt, b_re, b_im, c_re, c_im, d_skip, w_glu, b_glu,
              conv_w, a_log, dt_bias, gdn_norm, w_out):
    bsz, seq, _ = h.shape
    heads = lambda t, n: t.reshape(bsz, seq, n, -1)
    proj = h @ w_in
    su, qkv, z, da, db = _split(proj, ODD_SPLITS)
    o_c = s5_ssm(su, a_re, a_im, log_dt, b_re, b_im, c_re, c_im, d_skip, w_glu, b_glu).astype(h.dtype)
    qkv = jax.nn.silu(causal_dwconv(qkv, conv_w))
    dq, dk_, dv = _split(qkv, [GDN_W, GDN_W, GDN_W])
    q = l2_norm(heads(dq, GDN_HEADS))
    k = l2_norm(heads(dk_, GDN_HEADS))
    log_alpha = -jnp.exp(a_log.astype(jnp.float32)) * jax.nn.softplus(
        da.astype(jnp.float32) + dt_bias.astype(jnp.float32))
    beta = jax.nn.sigmoid(db.astype(jnp.float32))
    o_d = gated_delta_rule(q, k, heads(dv, GDN_HEADS), log_alpha, beta)
    o_d = rms_norm(o_d, gdn_norm) * jax.nn.silu(heads(z, GDN_HEADS).astype(jnp.float32))
    o_d = o_d.reshape(bsz, seq, GDN_W).astype(h.dtype)
    return jnp.concatenate([o_c, o_d], axis=-1) @ w_out


def conv_ffn(h, w_up, conv_w, conv_b, w_down):
    u = causal_dwconv(h @ w_up, conv_w) + conv_b
    a, g = jnp.split(u, 2, axis=-1)
    return (jax.nn.silu(g) * a) @ w_down


def setup_inputs(seed: int = 0) -> dict:
    key = jax.random.key(seed)
    ks = iter(jax.random.split(key, 48))
    nrm = lambda shape, scale: jax.random.normal(next(ks), shape, jnp.float32) * scale
    uni = lambda shape, lo, hi: jax.random.uniform(next(ks), shape, jnp.float32, lo, hi)
    gain = lambda shape: 1.0 + nrm(shape, 0.02)
    dt_g = jnp.exp(uni((N_ODD, GDN_HEADS), math.log(1e-3), math.log(1e-1)))
    return {
        "x": nrm((BATCH, SEQ, D_MODEL), 1.0),
        "norm_mix": gain((DEPTH, D_MODEL)),
        "norm_ffn": gain((DEPTH, D_MODEL)),
        "w_in_even": nrm((N_EVEN, D_MODEL, EVEN_IN), D_MODEL ** -0.5),
        "w_gla_gate": nrm((N_EVEN, GLA_RANK, GLA_W), GLA_RANK ** -0.5),
        "b_gla_gate": nrm((N_EVEN, GLA_W), 0.1),
        "gla_out_norm": gain((N_EVEN, GLA_DV)),
        "swa_q_norm": gain((N_EVEN, HEAD_DIM)),
        "swa_k_norm": gain((N_EVEN, HEAD_DIM)),
        "swa_sinks": nrm((N_EVEN, SWA_Q_HEADS), 0.5),
        "w_out_even": nrm((N_EVEN, GLA_W + SWA_W, D_MODEL), (GLA_W + SWA_W) ** -0.5),
        "w_in_odd": nrm((N_ODD, D_MODEL, ODD_IN), D_MODEL ** -0.5),
        "s5_a_re": -0.5 + nrm((N_ODD, S5_GROUPS, S5_STATE), 0.01),
        "s5_a_im": math.pi * jnp.arange(S5_STATE, dtype=jnp.float32) + nrm((N_ODD, S5_GROUPS, S5_STATE), 0.01),
        "s5_log_dt": uni((N_ODD, S5_GROUPS), math.log(1e-3), math.log(1e-1)),
        "s5_b_re": nrm((N_ODD, S5_GROUPS, S5_STATE, S5_GROUP), (2 * S5_GROUP) ** -0.5),
        "s5_b_im": nrm((N_ODD, S5_GROUPS, S5_STATE, S5_GROUP), (2 * S5_GROUP) ** -0.5),
        "s5_c_re": nrm((N_ODD, S5_GROUPS, S5_GROUP, S5_STATE), S5_STATE ** -0.5),
        "s5_c_im": nrm((N_ODD, S5_GROUPS, S5_GROUP, S5_STATE), S5_STATE ** -0.5),
        "s5_d": nrm((N_ODD, S5_W), 1.0),
        "s5_w_glu": nrm((N_ODD, S5_W, S5_W), S5_W ** -0.5),
        "s5_b_glu": nrm((N_ODD, S5_W), 0.01),
        "gdn_conv_w": nrm((N_ODD, GDN_CONV, 3 * GDN_W), GDN_CONV ** -0.5),
        "gdn_a_log": jnp.log(uni((N_ODD, GDN_HEADS), 1.0, 16.0)),
        "gdn_dt_bias": dt_g + jnp.log(-jnp.expm1(-dt_g)),
        "gdn_out_norm": gain((N_ODD, GDN_DV)),
        "w_out_odd": nrm((N_ODD, S5_W + GDN_W, D_MODEL), (S5_W + GDN_W) ** -0.5),
        "w_ffn_up": nrm((DEPTH, D_MODEL, 2 * FFN_DIM), D_MODEL ** -0.5),
        "ffn_conv_w": nrm((DEPTH, FFN_CONV, 2 * FFN_DIM), FFN_CONV ** -0.5),
        "ffn_conv_b": nrm((DEPTH, 2 * FFN_DIM), 0.01),
        "w_ffn_down": nrm((DEPTH, FFN_DIM, D_MODEL), FFN_DIM ** -0.5),
    }


def reference(x, norm_mix, norm_ffn, w_in_even, w_gla_gate, b_gla_gate, gla_out_norm,
              swa_q_norm, swa_k_norm, swa_sinks, w_out_even, w_in_odd, s5_a_re, s5_a_im,
              s5_log_dt, s5_b_re, s5_b_im, s5_c_re, s5_c_im, s5_d, s5_w_glu, s5_b_glu,
              gdn_conv_w, gdn_a_log, gdn_dt_bias, gdn_out_norm, w_out_odd,
              w_ffn_up, ffn_conv_w, ffn_conv_b, w_ffn_down):
    slopes = alibi_slopes(SWA_Q_HEADS)
    for layer in range(DEPTH):
        i = layer // 2
        h = rms_norm(x, norm_mix[layer])
        if layer % 2 == 0:
            mix = even_mixer(h, w_in_even[i], w_gla_gate[i], b_gla_gate[i], gla_out_norm[i],
                             swa_q_norm[i], swa_k_norm[i], swa_sinks[i], w_out_even[i], slopes)
        else:
            mix = odd_mixer(h, w_in_odd[i], s5_a_re[i], s5_a_im[i], s5_log_dt[i], s5_b_re[i],
                            s5_b_im[i], s5_c_re[i], s5_c_im[i], s5_d[i], s5_w_glu[i], s5_b_glu[i],
                            gdn_conv_w[i], gdn_a_log[i], gdn_dt_bias[i], gdn_out_norm[i], w_out_odd[i])
        x = x + mix.astype(x.dtype)
        h = rms_norm(x, norm_ffn[layer])
        x = x + conv_ffn(h, w_ffn_up[layer], ffn_conv_w[layer], ffn_conv_b[layer],
                         w_ffn_down[layer]).astype(x.dtype)
    return x
```

```python
import functools
import math

import jax
import jax.numpy as jnp
from jax import lax
from jax.experimental import pallas as pl
from jax.experimental.pallas import tpu as pltpu

F32 = jnp.float32
BF16 = jnp.bfloat16

D_MODEL = 1024
HEAD_DIM = 64
HEADS = 8
MIX_W = HEADS * HEAD_DIM
PAIRS = HEADS // 2
PAIR_W = 2 * HEAD_DIM
GLA_RANK = 16
GLA_GATE_NORM = 16.0
CHUNK = 64
SWA_BLOCK = 128
SWA_KV_HEADS = 2
S5_GROUP = 16
S5_GROUPS = 32
S5_STATE = 64
S5_LANES = S5_GROUPS * S5_STATE
S5_LANE_BLOCKS = S5_LANES // 128
GDN_CONV = 4
FFN_DIM = 2816
FFN_CONV = 3
NORM_EPS = 1e-6
LANES = 128
SUBLANES = 8
MXU_COLS = 256
VMEM_LIMIT = 56 * 1024 * 1024

MIX_TILE = 256
FFN_TILE = 512
FFN_COLS = MXU_COLS
SCAN_PAD = MIX_TILE // 2
SCAN_LEVELS = int(math.log2(MIX_TILE))


def _dot(a, b):
    return jnp.dot(a, b, preferred_element_type=F32)


def _dot_nt(a, b):
    return lax.dot_general(a, b, (((1,), (1,)), ((), ())), preferred_element_type=F32)


def _dot_tn(a, b):
    return lax.dot_general(a, b, (((0,), (0,)), ((), ())), preferred_element_type=F32)


def _split3(x):
    hi = x.astype(BF16)
    r1 = x - hi.astype(F32)
    mid = r1.astype(BF16)
    lo = (r1 - mid.astype(F32)).astype(BF16)
    return hi, mid, lo


def _dot01_lhs(m01, x):
    hi, mid, lo = _split3(x)
    return _dot(m01, hi) + _dot(m01, mid) + _dot(m01, lo)


def _dot01_rhs(x, m01):
    hi, mid, lo = _split3(x)
    return _dot(hi, m01) + _dot(mid, m01) + _dot(lo, m01)


def _sigmoid(x):
    return 1.0 / (1.0 + jnp.exp(-x))


def _silu(x):
    return x * _sigmoid(x)


def _softplus(x):
    return jnp.maximum(x, 0.0) + jnp.log1p(jnp.exp(-jnp.abs(x)))


def _rms_rows(x, gain_row):
    ms = jnp.mean(x * x, axis=-1, keepdims=True)
    return x * lax.rsqrt(ms + NORM_EPS) * gain_row


def _lane_half_masks(rows, dtype):
    lane = lax.broadcasted_iota(jnp.int32, (rows, PAIR_W), 1)
    return lane < HEAD_DIM


def _block_diag_pair(v):
    first = _lane_half_masks(v.shape[0], v.dtype)
    zero = jnp.zeros_like(v)
    return jnp.concatenate([jnp.where(first, v, zero), jnp.where(first, zero, v)], axis=0)


def _const_spec(shape):
    nd = len(shape)
    return pl.BlockSpec(shape, lambda b, l: (0,) * nd, pipeline_mode=pl.Buffered(1))


def _tile_spec(tile, width):
    return pl.BlockSpec((1, tile, width), lambda b, l: (b, l, 0))


def _params():
    return pltpu.CompilerParams(dimension_semantics=("arbitrary", "arbitrary"),
                                vmem_limit_bytes=VMEM_LIMIT)


def _ffn_kernel(x_ref, gain_ref, wa_ref, wg_ref, cwa_ref, cwg_ref, cba_ref, cbg_ref, wd_ref,
                o_ref, ua_ref, ug_ref, act_ref):
    tile = x_ref.shape[1]

    @pl.when(pl.program_id(1) == 0)
    def _():
        ua_ref[0:SUBLANES, :] = jnp.zeros((SUBLANES, FFN_DIM), F32)
        ug_ref[0:SUBLANES, :] = jnp.zeros((SUBLANES, FFN_DIM), F32)

    x = x_ref[0]
    h = _rms_rows(x, gain_ref[...]).astype(BF16)
    for c in range(FFN_DIM // FFN_COLS):
        sl = slice(c * FFN_COLS, (c + 1) * FFN_COLS)
        ua_ref[SUBLANES:SUBLANES + tile, sl] = _dot(h, wa_ref[:, sl])
        ug_ref[SUBLANES:SUBLANES + tile, sl] = _dot(h, wg_ref[:, sl])
        a = cba_ref[:, sl]
        g = cbg_ref[:, sl]
        for k in range(FFN_CONV):
            off = SUBLANES - (FFN_CONV - 1) + k
            a = a + cwa_ref[k:k + 1, sl] * ua_ref[off:off + tile, sl]
            g = g + cwg_ref[k:k + 1, sl] * ug_ref[off:off + tile, sl]
        act_ref[:, sl] = (_silu(g) * a).astype(BF16)
    ua_ref[0:SUBLANES, :] = ua_ref[tile:tile + SUBLANES, :]
    ug_ref[0:SUBLANES, :] = ug_ref[tile:tile + SUBLANES, :]
    o_ref[0] = x + _dot(act_ref[...], wd_ref[...])


def _ffn_layer(x, gain, w_up, conv_w, conv_b, w_down):
    bsz, seq, _ = x.shape
    tile = min(FFN_TILE, seq)
    wa = w_up[:, :FFN_DIM].astype(BF16)
    wg = w_up[:, FFN_DIM:].astype(BF16)
    args = (x, gain.reshape(1, D_MODEL), wa, wg, conv_w[:, :FFN_DIM], conv_w[:, FFN_DIM:],
            conv_b[:FFN_DIM].reshape(1, FFN_DIM), conv_b[FFN_DIM:].reshape(1, FFN_DIM),
            w_down.astype(BF16))
    in_specs = [_tile_spec(tile, D_MODEL)] + [_const_spec(a.shape) for a in args[1:]]
    return pl.pallas_call(
        _ffn_kernel,
        grid=(bsz, seq // tile),
        in_specs=in_specs,
        out_specs=_tile_spec(tile, D_MODEL),
        out_shape=jax.ShapeDtypeStruct(x.shape, F32),
        scratch_shapes=[pltpu.VMEM((SUBLANES + tile, FFN_DIM), F32),
                        pltpu.VMEM((SUBLANES + tile, FFN_DIM), F32),
                        pltpu.VMEM((tile, FFN_DIM), BF16)],
        compiler_params=_params(),
        name="conv_ffn",
    )(*args)


def _seg_mean(y, seg01):
    hi = y.astype(BF16)
    lo = (y - hi.astype(F32)).astype(BF16)
    return (_dot(hi, seg01) + _dot(lo, seg01)) * (1.0 / HEAD_DIM)


def _even_kernel(sinks_ref, x_ref, gain_ref, wmain_ref, wkv_ref, wglr_ref, wgate_ref, bgate_ref,
                 glan_ref, qn_ref, kn_ref, seg_ref, tril_ref, woa_ref, wob_ref,
                 o_ref, st_ref, kvs_ref, oa_ref, ob_ref):
    tile = x_ref.shape[1]
    seq_tile = pl.program_id(1)

    @pl.when(seq_tile == 0)
    def _():
        st_ref[...] = jnp.zeros_like(st_ref)
        kvs_ref[0:SWA_BLOCK, :] = jnp.zeros((SWA_BLOCK, 2 * PAIR_W), F32)

    x = x_ref[0]
    h = _rms_rows(x, gain_ref[...]).astype(BF16)
    proj = _dot(h, wmain_ref[...])
    gq = proj[:, 0 * MIX_W:1 * MIX_W]
    gk = proj[:, 1 * MIX_W:2 * MIX_W]
    gv = proj[:, 2 * MIX_W:3 * MIX_W]
    gr = proj[:, 3 * MIX_W:4 * MIX_W]
    sq = proj[:, 4 * MIX_W:5 * MIX_W]
    kv = _dot(h, wkv_ref[...])
    glr = _dot(h, wglr_ref[...])
    seg = seg_ref[...]

    gate = _dot(glr.astype(BF16), wgate_ref[...]) + bgate_ref[...]
    log_g = (jnp.minimum(gate, 0.0) - jnp.log1p(jnp.exp(-jnp.abs(gate)))) * (1.0 / GLA_GATE_NORM)
    cum = _dot01_lhs(tril_ref[...], log_g)
    q_pos = (gq * HEAD_DIM ** -0.5) * jnp.exp(cum)
    k_neg = gk * jnp.exp(-cum)
    row = lax.broadcasted_iota(jnp.int32, (CHUNK, PAIR_W), 0)
    lane = lax.broadcasted_iota(jnp.int32, (CHUNK, PAIR_W), 1)
    causal = row >= (lane % CHUNK)
    bd_row = lax.broadcasted_iota(jnp.int32, (PAIR_W, PAIR_W), 0)
    bd_lane = lax.broadcasted_iota(jnp.int32, (PAIR_W, PAIR_W), 1)
    same_head = (bd_row < HEAD_DIM) == (bd_lane < HEAD_DIM)
    for c in range(tile // CHUNK):
        rs = slice(c * CHUNK, (c + 1) * CHUNK)
        cum_c = cum[rs]
        cum_last = cum_c[CHUNK - 1:CHUNK]
        k_dec = gk[rs] * jnp.exp(cum_last - cum_c)
        dec = jnp.exp(cum_last)
        for p in range(PAIRS):
            ls = slice(p * PAIR_W, (p + 1) * PAIR_W)
            qp = q_pos[rs, ls].astype(BF16)
            vp = gv[rs, ls].astype(BF16)
            scores = _dot_nt(qp, _block_diag_pair(k_neg[rs, ls].astype(BF16)))
            scores = jnp.where(causal, scores, 0.0).astype(BF16)
            state_t = st_ref[p]
            o = _dot(scores, _block_diag_pair(vp)) + _dot_nt(qp, state_t.astype(BF16))
            oa_ref[rs, ls] = o
            upd = _dot_tn(vp, k_dec[:, ls].astype(BF16))
            st_ref[p] = state_t * dec[:, ls] + jnp.where(same_head, upd, 0.0)
    o_a = oa_ref[...]
    o_a = o_a * lax.rsqrt(_seg_mean(o_a * o_a, seg) + NORM_EPS) * glan_ref[...] * _silu(gr)

    qn = sq * lax.rsqrt(_seg_mean(sq * sq, seg) + NORM_EPS) * qn_ref[...]
    sk = kv[:, 0:PAIR_W]
    kn = sk * lax.rsqrt(_seg_mean(sk * sk, seg[0:PAIR_W, 0:PAIR_W]) + NORM_EPS) * kn_ref[...]
    kvs_ref[SWA_BLOCK:SWA_BLOCK + tile, 0:PAIR_W] = kn
    kvs_ref[SWA_BLOCK:SWA_BLOCK + tile, PAIR_W:2 * PAIR_W] = kv[:, PAIR_W:2 * PAIR_W]
    srow = lax.broadcasted_iota(jnp.int32, (2 * SWA_BLOCK, 2 * SWA_BLOCK), 0)
    skey = lax.broadcasted_iota(jnp.int32, (2 * SWA_BLOCK, 2 * SWA_BLOCK), 1)
    dist = (srow % SWA_BLOCK) + SWA_BLOCK - skey
    valid = (dist >= 0) & (dist < SWA_BLOCK)
    dist_f = dist.astype(F32)
    top = srow < SWA_BLOCK
    top_col = top[:, 0:1]
    ones_cols = jnp.ones((2 * SWA_BLOCK, PAIR_W), BF16)
    first_lane = lax.broadcasted_iota(jnp.int32, (2 * SWA_BLOCK, PAIR_W), 1) < HEAD_DIM
    no_prev = jnp.where(seq_tile > 0, 0.0, -jnp.inf)
    for j in range(tile // SWA_BLOCK):
        ks = slice(j * SWA_BLOCK, (j + 2) * SWA_BLOCK)
        keys = kvs_ref[ks, 0:PAIR_W].astype(BF16)
        vals = kvs_ref[ks, PAIR_W:2 * PAIR_W].astype(BF16)
        zero_v = jnp.zeros_like(vals)
        vals0 = jnp.concatenate([jnp.where(first_lane, vals, zero_v), ones_cols], axis=1)
        vals1 = jnp.concatenate([jnp.where(first_lane, zero_v, vals), ones_cols], axis=1)
        for p in range(PAIRS):
            qs = _block_diag_pair(qn[j * SWA_BLOCK:(j + 1) * SWA_BLOCK,
                                     p * PAIR_W:(p + 1) * PAIR_W].astype(BF16))
            slope = jnp.where(top, 2.0 ** -(p + 1), 2.0 ** -(p + 1 + PAIRS))
            s = _dot_nt(qs, keys) * HEAD_DIM ** -0.5 - slope * dist_f
            s = jnp.where(valid, s, -jnp.inf)
            if j == 0:
                s = s + jnp.where(skey < SWA_BLOCK, no_prev, 0.0)
            sink = jnp.where(top_col, sinks_ref[p], sinks_ref[p + PAIRS])
            m = jnp.maximum(jnp.max(s, axis=-1, keepdims=True), sink)
            pe = jnp.exp(s - m).astype(BF16)
            sink_e = jnp.exp(sink - m)
            r0 = _dot(pe[0:SWA_BLOCK], vals0)
            r1 = _dot(pe[SWA_BLOCK:], vals1)
            o_pair = (r0[:, 0:PAIR_W] / (r0[:, PAIR_W:] + sink_e[0:SWA_BLOCK])
                      + r1[:, 0:PAIR_W] / (r1[:, PAIR_W:] + sink_e[SWA_BLOCK:]))
            ob_ref[j * SWA_BLOCK:(j + 1) * SWA_BLOCK, p * PAIR_W:(p + 1) * PAIR_W] = o_pair
    kvs_ref[0:SWA_BLOCK, :] = kvs_ref[tile:tile + SWA_BLOCK, :]

    out = _dot(o_a.astype(BF16), woa_ref[...]) + _dot(ob_ref[...].astype(BF16), wob_ref[...])
    o_ref[0] = x + out


def _block_tril(tile):
    r = jnp.arange(tile)
    same = (r[:, None] // CHUNK) == (r[None, :] // CHUNK)
    return (same & (r[:, None] >= r[None, :])).astype(BF16)


def _seg_matrix():
    r = jnp.arange(MIX_W)
    return ((r[:, None] // HEAD_DIM) == (r[None, :] // HEAD_DIM)).astype(BF16)


def _head_tile(v, reps):
    return jnp.tile(v.astype(F32), reps).reshape(1, reps * v.shape[0])


def _even_layer(x, gain, w_in, w_gate, b_gate, gla_norm, q_norm, k_norm, sinks, w_out):
    bsz, seq, _ = x.shape
    tile = min(MIX_TILE, seq)
    order = jnp.array([h for p in range(PAIRS) for h in (p, p + PAIRS)])
    c0 = 4 * MIX_W
    glr0 = c0
    sq0 = c0 + GLA_RANK
    sk0 = sq0 + MIX_W
    w_sq = w_in[:, sq0:sk0].reshape(D_MODEL, HEADS, HEAD_DIM)[:, order].reshape(D_MODEL, MIX_W)
    w_main = jnp.concatenate([w_in[:, :c0], w_sq], axis=1).astype(BF16)
    w_kv = w_in[:, sk0:].astype(BF16)
    w_glr = jnp.pad(w_in[:, glr0:sq0], ((0, 0), (0, LANES - GLA_RANK))).astype(BF16)
    w_gate_p = jnp.pad(w_gate, ((0, LANES - GLA_RANK), (0, 0))).astype(BF16)
    w_out_a = w_out[:MIX_W].astype(BF16)
    w_out_b = w_out[MIX_W:].reshape(HEADS, HEAD_DIM, D_MODEL)[order].reshape(MIX_W, D_MODEL).astype(BF16)
    args = (x, gain.reshape(1, D_MODEL), w_main, w_kv, w_glr, w_gate_p, b_gate.reshape(1, MIX_W),
            _head_tile(gla_norm, HEADS), _head_tile(q_norm, HEADS), _head_tile(k_norm, SWA_KV_HEADS),
            _seg_matrix(), _block_tril(tile), w_out_a, w_out_b)
    in_specs = ([pl.BlockSpec(memory_space=pltpu.SMEM), _tile_spec(tile, D_MODEL)]
                + [_const_spec(a.shape) for a in args[1:]])
    return pl.pallas_call(
        _even_kernel,
        grid=(bsz, seq // tile),
        in_specs=in_specs,
        out_specs=_tile_spec(tile, D_MODEL),
        out_shape=jax.ShapeDtypeStruct(x.shape, F32),
        scratch_shapes=[pltpu.VMEM((PAIRS, PAIR_W, PAIR_W), F32),
                        pltpu.VMEM((SWA_BLOCK + tile, 2 * PAIR_W), F32),
                        pltpu.VMEM((tile, MIX_W), F32),
                        pltpu.VMEM((tile, MIX_W), F32)],
        compiler_params=_params(),
        name="gla_swa_mixer",
    )(sinks.astype(F32), *args)


def _s5_prep_kernel(are_ref, aim_ref, ldt_ref, bre_ref, bim_ref, pre_ref, pim_ref, bbre_ref, bbim_ref):
    a_re = are_ref[...]
    a_im = aim_ref[...]
    dt = jnp.exp(ldt_ref[...])
    for k in range(SCAN_LEVELS):
        step = float(2 ** k)
        mag = jnp.exp(step * dt * a_re)
        ang = step * dt * a_im
        pre_ref[k] = mag * jnp.cos(ang)
        pim_ref[k] = mag * jnp.sin(ang)
    abar_re = pre_ref[0]
    abar_im = pim_ref[0]
    den = a_re * a_re + a_im * a_im
    f_re = ((abar_re - 1.0) * a_re + abar_im * a_im) / den
    f_im = (abar_im * a_re - (abar_re - 1.0) * a_im) / den
    for c in range(S5_GROUP):
        bbre_ref[c] = f_re * bre_ref[c] - f_im * bim_ref[c]
        bbim_ref[c] = f_re * bim_ref[c] + f_im * bre_ref[c]


def _s5_kernel(x_ref, gain_ref, wsu_ref, bre_ref, bim_ref, pre_ref, pim_ref, cre_ref, cim_ref,
               dskip_ref, wglu_ref, bglu_ref, o_ref, buf_ref, st_ref, xs_ref):
    tile = x_ref.shape[1]
    half_in = MIX_W // 2
    half_state = S5_LANES // 2

    @pl.when(pl.program_id(1) == 0)
    def _():
        st_ref[...] = jnp.zeros_like(st_ref)
        buf_ref[:, :, :, 0:SCAN_PAD, :] = jnp.zeros((2, 2, S5_LANE_BLOCKS, SCAN_PAD, LANES), F32)

    x = x_ref[0]
    h = _rms_rows(x, gain_ref[...]).astype(BF16)
    u = _dot(h, wsu_ref[...])
    ub = u.astype(BF16)
    for kt in range(2):
        u_half = ub[:, kt * half_in:(kt + 1) * half_in]
        bu_re = _dot(u_half, bre_ref[kt])
        bu_im = _dot(u_half, bim_ref[kt])
        for j in range(S5_LANE_BLOCKS // 2):
            lb = kt * (S5_LANE_BLOCKS // 2) + j
            buf_ref[0, 0, lb, SCAN_PAD:SCAN_PAD + tile, :] = bu_re[:, j * LANES:(j + 1) * LANES]
            buf_ref[0, 1, lb, SCAN_PAD:SCAN_PAD + tile, :] = bu_im[:, j * LANES:(j + 1) * LANES]

    def scan_block(lb, carry):
        p_re = pre_ref[lb]
        p_im = pim_ref[lb]
        s_re = st_ref[0, lb, 0:1, :]
        s_im = st_ref[1, lb, 0:1, :]
        r0 = buf_ref[0, 0, lb, SCAN_PAD:SCAN_PAD + 1, :]
        i0 = buf_ref[0, 1, lb, SCAN_PAD:SCAN_PAD + 1, :]
        buf_ref[0, 0, lb, SCAN_PAD:SCAN_PAD + 1, :] = r0 + p_re[0:1] * s_re - p_im[0:1] * s_im
        buf_ref[0, 1, lb, SCAN_PAD:SCAN_PAD + 1, :] = i0 + p_re[0:1] * s_im + p_im[0:1] * s_re
        for k in range(SCAN_LEVELS):
            src, dst = k % 2, 1 - k % 2
            shift = 2 ** k
            cur_re = buf_ref[src, 0, lb, SCAN_PAD:SCAN_PAD + tile, :]
            cur_im = buf_ref[src, 1, lb, SCAN_PAD:SCAN_PAD + tile, :]
            sh_re = buf_ref[src, 0, lb, SCAN_PAD - shift:SCAN_PAD - shift + tile, :]
            sh_im = buf_ref[src, 1, lb, SCAN_PAD - shift:SCAN_PAD - shift + tile, :]
            a_re = p_re[k:k + 1]
            a_im = p_im[k:k + 1]
            buf_ref[dst, 0, lb, SCAN_PAD:SCAN_PAD + tile, :] = cur_re + a_re * sh_re - a_im * sh_im
            buf_ref[dst, 1, lb, SCAN_PAD:SCAN_PAD + tile, :] = cur_im + a_re * sh_im + a_im * sh_re
        fin = SCAN_LEVELS % 2
        x_re = buf_ref[fin, 0, lb, SCAN_PAD:SCAN_PAD + tile, :]
        x_im = buf_ref[fin, 1, lb, SCAN_PAD:SCAN_PAD + tile, :]
        st_ref[0, lb, 0:1, :] = x_re[tile - 1:tile]
        st_ref[1, lb, 0:1, :] = x_im[tile - 1:tile]
        xs_ref[0, lb] = x_re.astype(BF16)
        xs_ref[1, lb] = x_im.astype(BF16)
        return carry

    lax.fori_loop(0, S5_LANE_BLOCKS, scan_block, 0)

    ys = []
    for kt in range(2):
        blocks = range(kt * (S5_LANE_BLOCKS // 2), (kt + 1) * (S5_LANE_BLOCKS // 2))
        x_re = jnp.concatenate([xs_ref[0, lb] for lb in blocks], axis=1)
        x_im = jnp.concatenate([xs_ref[1, lb] for lb in blocks], axis=1)
        ys.append(_dot(x_re, cre_ref[kt]) - _dot(x_im, cim_ref[kt]))
    y = jnp.concatenate(ys, axis=1) + dskip_ref[...] * u
    y = 0.5 * y * (1.0 + jnp.tanh(math.sqrt(2.0 / math.pi) * (y + 0.044715 * (y * y * y))))
    gate = _sigmoid(_dot(y.astype(BF16), wglu_ref[...]) + bglu_ref[...])
    o_ref[0] = y * gate


def _s5_branch(x, gain, w_su, a_re, a_im, log_dt, b_re, b_im, c_re, c_im, d_skip, w_glu, b_glu):
    bsz, seq, _ = x.shape
    tile = min(MIX_TILE, seq)
    assert tile == MIX_TILE, "S5 scan levels are sized for MIX_TILE rows"
    gs = (S5_GROUPS, S5_STATE)
    pow_re, pow_im, bb_re, bb_im = pl.pallas_call(
        _s5_prep_kernel,
        out_shape=[jax.ShapeDtypeStruct((SCAN_LEVELS,) + gs, F32)] * 2
        + [jax.ShapeDtypeStruct((S5_GROUP,) + gs, F32)] * 2,
        name="s5_discretize",
    )(a_re, a_im, log_dt.reshape(S5_GROUPS, 1), b_re.transpose(2, 0, 1), b_im.transpose(2, 0, 1))
    half_g = S5_GROUPS // 2
    eye = jnp.eye(half_g, dtype=F32)

    def in_blocks(bb):
        t = bb.reshape(S5_GROUP, 2, half_g, S5_STATE)
        return jnp.einsum('ckgp,gh->kgchp', t, eye).reshape(2, half_g * S5_GROUP, half_g * S5_STATE).astype(BF16)

    def out_blocks(cc):
        t = cc.reshape(2, half_g, S5_GROUP, S5_STATE)
        return jnp.einsum('kgop,gh->kgpho', t, eye).reshape(2, half_g * S5_STATE, half_g * S5_GROUP).astype(BF16)

    def lane_blocks(pw):
        return pw.reshape(SCAN_LEVELS, S5_LANE_BLOCKS, LANES).transpose(1, 0, 2)

    args = (x, gain.reshape(1, D_MODEL), w_su.astype(BF16), in_blocks(bb_re), in_blocks(bb_im),
            lane_blocks(pow_re), lane_blocks(pow_im), out_blocks(c_re), out_blocks(c_im),
            d_skip.reshape(1, MIX_W), w_glu.astype(BF16), b_glu.reshape(1, MIX_W))
    in_specs = [_tile_spec(tile, D_MODEL)] + [_const_spec(a.shape) for a in args[1:]]
    return pl.pallas_call(
        _s5_kernel,
        grid=(bsz, seq // tile),
        in_specs=in_specs,
        out_specs=_tile_spec(tile, MIX_W),
        out_shape=jax.ShapeDtypeStruct((bsz, seq, MIX_W), F32),
        scratch_shapes=[pltpu.VMEM((2, 2, S5_LANE_BLOCKS, SCAN_PAD + tile, LANES), F32),
                        pltpu.VMEM((2, S5_LANE_BLOCKS, SUBLANES, LANES), F32),
                        pltpu.VMEM((2, S5_LANE_BLOCKS, tile, LANES), BF16)],
        compiler_params=_params(),
        name="s5_branch",
    )(*args)


def _pair_matmul(a, b):
    return _dot(a.astype(BF16), _block_diag_pair(b.astype(BF16)))


def _gdn_kernel(x_ref, oc_ref, gain_ref, wqkvz_ref, wda_ref, wdb_ref, cw_ref, alog_ref, dtb_ref,
                gnorm_ref, seg_ref, tril_ref, onesbd_ref, woc_ref, wod_ref,
                o_ref, st_ref, cs_ref, od_ref):
    tile = x_ref.shape[1]
    qkv_w = 3 * MIX_W

    @pl.when(pl.program_id(1) == 0)
    def _():
        st_ref[...] = jnp.zeros_like(st_ref)
        cs_ref[0:SUBLANES, :] = jnp.zeros((SUBLANES, qkv_w), F32)

    x = x_ref[0]
    h = _rms_rows(x, gain_ref[...]).astype(BF16)
    proj = _dot(h, wqkvz_ref[...])
    z = proj[:, qkv_w:]
    da = _dot(h, wda_ref[...])
    db = _dot(h, wdb_ref[...])
    seg = seg_ref[...]

    cs_ref[SUBLANES:SUBLANES + tile, :] = proj[:, :qkv_w]
    conv = jnp.zeros((tile, qkv_w), F32)
    for k in range(GDN_CONV):
        off = SUBLANES - (GDN_CONV - 1) + k
        conv = conv + cw_ref[k:k + 1, :] * cs_ref[off:off + tile, :]
    cs_ref[0:SUBLANES, :] = cs_ref[tile:tile + SUBLANES, :]
    qkv = _silu(conv)
    q = qkv[:, 0:MIX_W]
    k_ = qkv[:, MIX_W:2 * MIX_W]
    v = qkv[:, 2 * MIX_W:]
    q = q * lax.rsqrt(_seg_mean(q * q, seg) * HEAD_DIM + NORM_EPS) * HEAD_DIM ** -0.5
    k_ = k_ * lax.rsqrt(_seg_mean(k_ * k_, seg) * HEAD_DIM + NORM_EPS)

    log_alpha = -jnp.exp(alog_ref[...]) * _softplus(da + dtb_ref[...])
    beta = _sigmoid(db)
    g = _dot01_lhs(tril_ref[...], log_alpha)
    row = lax.broadcasted_iota(jnp.int32, (tile, MIX_W), 0) % CHUNK
    key = lax.broadcasted_iota(jnp.int32, (tile, MIX_W), 1) % CHUNK
    g_key = _dot01_lhs(onesbd_ref[...], jnp.where(row <= key, log_alpha, 0.0))
    decay_incl = jnp.exp(jnp.where(row >= key, g - g_key, -jnp.inf))
    decay_strict = jnp.where(row > key, decay_incl, 0.0)
    exp_g = jnp.exp(g)
    v_beta = v * beta
    k_beta_g = k_ * beta * exp_g
    q_dec = q * exp_g

    eye_pair = (lax.broadcasted_iota(jnp.int32, (CHUNK, PAIR_W), 0)
                == lax.broadcasted_iota(jnp.int32, (CHUNK, PAIR_W), 1) % CHUNK).astype(F32)
    bd_row = lax.broadcasted_iota(jnp.int32, (PAIR_W, PAIR_W), 0)
    bd_lane = lax.broadcasted_iota(jnp.int32, (PAIR_W, PAIR_W), 1)
    same_head = (bd_row < HEAD_DIM) == (bd_lane < HEAD_DIM)
    first_lane = lax.broadcasted_iota(jnp.int32, (CHUNK, PAIR_W), 1) < HEAD_DIM
    for c in range(tile // CHUNK):
        rs = slice(c * CHUNK, (c + 1) * CHUNK)
        g_c = g[rs]
        g_last = g_c[CHUNK - 1:CHUNK]
        k_dec = k_[rs] * jnp.exp(g_last - g_c)
        dec = jnp.exp(g_last)
        for p in range(PAIRS):
            ls = slice(p * PAIR_W, (p + 1) * PAIR_W)
            kb = k_[rs, ls].astype(BF16)
            sc = _dot_nt(jnp.concatenate([kb, q[rs, ls].astype(BF16)], axis=0), _block_diag_pair(kb))
            lower = beta[rs, ls] * sc[0:CHUNK] * decay_strict[rs, ls]
            attn = sc[CHUNK:] * decay_incl[rs, ls]
            inv = eye_pair - lower
            power = lower
            for _ in range(int(math.log2(CHUNK)) - 1):
                power = _pair_matmul(power, power)
                inv = inv + _pair_matmul(inv, power)
            vb = v_beta[rs, ls].astype(BF16)
            kbg = k_beta_g[rs, ls].astype(BF16)
            zero = jnp.zeros_like(vb)
            rhs = jnp.concatenate(
                [jnp.concatenate([jnp.where(first_lane, vb, zero), jnp.where(first_lane, kbg, zero)], axis=1),
                 jnp.concatenate([jnp.where(first_lane, zero, vb), jnp.where(first_lane, zero, kbg)], axis=1)],
                axis=0)
            uw = _dot(inv.astype(BF16), rhs)
            u = uw[:, 0:PAIR_W]
            w = uw[:, PAIR_W:]
            state = st_ref[p]
            ws_qs = _dot(jnp.concatenate([w.astype(BF16), q_dec[rs, ls].astype(BF16)], axis=0),
                         state.astype(BF16))
            v_new = u - ws_qs[0:CHUNK]
            v_new_b = v_new.astype(BF16)
            od_ref[rs, ls] = ws_qs[CHUNK:] + _dot(attn.astype(BF16), _block_diag_pair(v_new_b))
            upd = _dot_tn(k_dec[:, ls].astype(BF16), v_new_b)
            st_ref[p] = state * dec[:, ls] + jnp.where(same_head, upd, 0.0)
    o_d = od_ref[...]
    o_d = o_d * lax.rsqrt(_seg_mean(o_d * o_d, seg) + NORM_EPS) * gnorm_ref[...] * _silu(z)
    out = _dot(oc_ref[0].astype(BF16), woc_ref[...]) + _dot(o_d.astype(BF16), wod_ref[...])
    o_ref[0] = x + out


def _block_ones(tile):
    r = jnp.arange(tile)
    return ((r[:, None] // CHUNK) == (r[None, :] // CHUNK)).astype(BF16)


def _gdn_layer(x, o_c, gain, w_qkvz, w_da, w_db, conv_w, a_log, dt_bias, gdn_norm, w_out):
    bsz, seq, _ = x.shape
    tile = min(MIX_TILE, seq)
    rep = lambda w: jnp.repeat(w, HEAD_DIM, axis=-1)
    args = (x, o_c, gain.reshape(1, D_MODEL), w_qkvz.astype(BF16), rep(w_da).astype(BF16),
            rep(w_db).astype(BF16), conv_w, rep(a_log).reshape(1, MIX_W), rep(dt_bias).reshape(1, MIX_W),
            _head_tile(gdn_norm, HEADS), _seg_matrix(), _block_tril(tile), _block_ones(tile),
            w_out[:MIX_W].astype(BF16), w_out[MIX_W:].astype(BF16))
    in_specs = ([_tile_spec(tile, D_MODEL), _tile_spec(tile, MIX_W)]
                + [_const_spec(a.shape) for a in args[2:]])
    return pl.pallas_call(
        _gdn_kernel,
        grid=(bsz, seq // tile),
        in_specs=in_specs,
        out_specs=_tile_spec(tile, D_MODEL),
        out_shape=jax.ShapeDtypeStruct(x.shape, F32),
        scratch_shapes=[pltpu.VMEM((PAIRS, PAIR_W, PAIR_W), F32),
                        pltpu.VMEM((SUBLANES + tile, 3 * MIX_W), F32),
                        pltpu.VMEM((tile, MIX_W), F32)],
        compiler_params=_params(),
        name="gdn_mixer",
    )(*args)


def _odd_layer(x, gain, w_in, a_re, a_im, log_dt, b_re, b_im, c_re, c_im, d_skip, w_glu, b_glu,
               conv_w, a_log, dt_bias, gdn_norm, w_out):
    su1 = MIX_W
    z1 = su1 + 4 * MIX_W
    o_c = _s5_branch(x, gain, w_in[:, :su1], a_re, a_im, log_dt, b_re, b_im, c_re, c_im,
                     d_skip, w_glu, b_glu)
    return _gdn_layer(x, o_c, gain, w_in[:, su1:z1], w_in[:, z1:z1 + HEADS], w_in[:, z1 + HEADS:],
                      conv_w, a_log, dt_bias, gdn_norm, w_out)


def kernel(x, norm_mix, norm_ffn, w_in_even, w_gla_gate, b_gla_gate, gla_out_norm, swa_q_norm, swa_k_norm, swa_sinks, w_out_even, w_in_odd, s5_a_re, s5_a_im, s5_log_dt, s5_b_re, s5_b_im, s5_c_re, s5_c_im, s5_d, s5_w_glu, s5_b_glu, gdn_conv_w, gdn_a_log, gdn_dt_bias, gdn_out_norm, w_out_odd, w_ffn_up, ffn_conv_w, ffn_conv_b, w_ffn_down):
    depth = norm_mix.shape[0]
    for layer in range(depth):
        i = layer // 2
        if layer % 2 == 0:
            x = _even_layer(x, norm_mix[layer], w_in_even[i], w_gla_gate[i], b_gla_gate[i],
                            gla_out_norm[i], swa_q_norm[i], swa_k_norm[i], swa_sinks[i], w_out_even[i])
        else:
            x = _odd_layer(x, norm_mix[layer], w_in_odd[i], s5_a_re[i], s5_a_im[i], s5_log_dt[i],
                           s5_b_re[i], s5_b_im[i], s5_c_re[i], s5_c_im[i], s5_d[i], s5_w_glu[i],
                           s5_b_glu[i], gdn_conv_w[i], gdn_a_log[i], gdn_dt_bias[i], gdn_out_norm[i],
                           w_out_odd[i])
        x = _ffn_layer(x, norm_ffn[layer], w_ffn_up[layer], ffn_conv_w[layer], ffn_conv_b[layer],
                       w_ffn_down[layer])
    return x
```

```python
import functools
import math

import jax
import jax.numpy as jnp
from jax import lax
from jax.experimental import pallas as pl
from jax.experimental.pallas import tpu as pltpu

F32 = jnp.float32
BF16 = jnp.bfloat16

D_MODEL = 1024
HEAD_DIM = 64
HEADS = 8
MIX_W = HEADS * HEAD_DIM
PAIRS = HEADS // 2
PAIR_W = 2 * HEAD_DIM
GLA_RANK = 16
GLA_GATE_NORM = 16.0
CHUNK = 64
SWA_BLOCK = 128
SWA_KV_HEADS = 2
S5_GROUP = 16
S5_GROUPS = 32
S5_STATE = 64
S5_LANES = S5_GROUPS * S5_STATE
S5_LANE_BLOCKS = S5_LANES // 128
GDN_CONV = 4
FFN_DIM = 2816
FFN_CONV = 3
NORM_EPS = 1e-6
LANES = 128
SUBLANES = 8
MXU_COLS = 256
VMEM_LIMIT = 56 * 1024 * 1024

MIX_TILE = 256
FFN_TILE = 512
FFN_COLS = MXU_COLS
SCAN_PAD = MIX_TILE // 2
SCAN_LEVELS = int(math.log2(MIX_TILE))


def _dot(a, b):
    return jnp.dot(a, b, preferred_element_type=F32)


def _dot_nt(a, b):
    return lax.dot_general(a, b, (((1,), (1,)), ((), ())), preferred_element_type=F32)


def _dot_tn(a, b):
    return lax.dot_general(a, b, (((0,), (0,)), ((), ())), preferred_element_type=F32)


def _split3(x):
    hi = x.astype(BF16)
    r1 = x - hi.astype(F32)
    mid = r1.astype(BF16)
    lo = (r1 - mid.astype(F32)).astype(BF16)
    return hi, mid, lo


def _dot01_lhs(m01, x):
    hi, mid, lo = _split3(x)
    return _dot(m01, hi) + _dot(m01, mid) + _dot(m01, lo)


def _dot01_rhs(x, m01):
    hi, mid, lo = _split3(x)
    return _dot(hi, m01) + _dot(mid, m01) + _dot(lo, m01)


def _sigmoid(x):
    return 1.0 / (1.0 + jnp.exp(-x))


def _silu(x):
    return x * _sigmoid(x)


def _softplus(x):
    return jnp.maximum(x, 0.0) + jnp.log1p(jnp.exp(-jnp.abs(x)))


def _rms_rows(x, gain_row):
    ms = jnp.mean(x * x, axis=-1, keepdims=True)
    return x * lax.rsqrt(ms + NORM_EPS) * gain_row


def _lane_half_masks(rows, dtype):
    lane = lax.broadcasted_iota(jnp.int32, (rows, PAIR_W), 1)
    return lane < HEAD_DIM


def _block_diag_pair(v):
    first = _lane_half_masks(v.shape[0], v.dtype)
    zero = jnp.zeros_like(v)
    return jnp.concatenate([jnp.where(first, v, zero), jnp.where(first, zero, v)], axis=0)


def _const_spec(shape):
    nd = len(shape)
    return pl.BlockSpec(shape, lambda b, l: (0,) * nd, pipeline_mode=pl.Buffered(1))


def _tile_spec(tile, width):
    return pl.BlockSpec((1, tile, width), lambda b, l: (b, l, 0))


def _params():
    return pltpu.CompilerParams(dimension_semantics=("arbitrary", "arbitrary"),
                                vmem_limit_bytes=VMEM_LIMIT)


def _ffn_kernel(x_ref, gain_ref, wa_ref, wg_ref, cwa_ref, cwg_ref, cba_ref, cbg_ref, wd_ref,
                o_ref, ua_ref, ug_ref, act_ref):
    tile = x_ref.shape[1]

    @pl.when(pl.program_id(1) == 0)
    def _():
        ua_ref[0:SUBLANES, :] = jnp.zeros((SUBLANES, FFN_DIM), F32)
        ug_ref[0:SUBLANES, :] = jnp.zeros((SUBLANES, FFN_DIM), F32)

    x = x_ref[0]
    h = _rms_rows(x, gain_ref[...]).astype(BF16)
    for c in range(FFN_DIM // FFN_COLS):
        sl = slice(c * FFN_COLS, (c + 1) * FFN_COLS)
        ua_ref[SUBLANES:SUBLANES + tile, sl] = _dot(h, wa_ref[:, sl])
        ug_ref[SUBLANES:SUBLANES + tile, sl] = _dot(h, wg_ref[:, sl])
        a = cba_ref[:, sl]
        g = cbg_ref[:, sl]
        for k in range(FFN_CONV):
            off = SUBLANES - (FFN_CONV - 1) + k
            a = a + cwa_ref[k:k + 1, sl] * ua_ref[off:off + tile, sl]
            g = g + cwg_ref[k:k + 1, sl] * ug_ref[off:off + tile, sl]
        act_ref[:, sl] = (_silu(g) * a).astype(BF16)
    ua_ref[0:SUBLANES, :] = ua_ref[tile:tile + SUBLANES, :]
    ug_ref[0:SUBLANES, :] = ug_ref[tile:tile + SUBLANES, :]
    o_ref[0] = x + _dot(act_ref[...], wd_ref[...])


def _ffn_layer(x, gain, w_up, conv_w, conv_b, w_down):
    bsz, seq, _ = x.shape
    tile = min(FFN_TILE, seq)
    wa = w_up[:, :FFN_DIM].astype(BF16)
    wg = w_up[:, FFN_DIM:].astype(BF16)
    args = (x, gain.reshape(1, D_MODEL), wa, wg, conv_w[:, :FFN_DIM], conv_w[:, FFN_DIM:],
            conv_b[:FFN_DIM].reshape(1, FFN_DIM), conv_b[FFN_DIM:].reshape(1, FFN_DIM),
            w_down.astype(BF16))
    in_specs = [_tile_spec(tile, D_MODEL)] + [_const_spec(a.shape) for a in args[1:]]
    return pl.pallas_call(
        _ffn_kernel,
        grid=(bsz, seq // tile),
        in_specs=in_specs,
        out_specs=_tile_spec(tile, D_MODEL),
        out_shape=jax.ShapeDtypeStruct(x.shape, F32),
        scratch_shapes=[pltpu.VMEM((SUBLANES + tile, FFN_DIM), F32),
                        pltpu.VMEM((SUBLANES + tile, FFN_DIM), F32),
                        pltpu.VMEM((tile, FFN_DIM), BF16)],
        compiler_params=_params(),
        name="conv_ffn",
    )(*args)


def _seg_mean(y, seg01):
    hi = y.astype(BF16)
    lo = (y - hi.astype(F32)).astype(BF16)
    return (_dot(hi, seg01) + _dot(lo, seg01)) * (1.0 / HEAD_DIM)


def _even_kernel(sinks_ref, x_ref, gain_ref, wmain_ref, wkv_ref, wglr_ref, wgate_ref, bgate_ref,
                 glan_ref, qn_ref, kn_ref, seg_ref, tril_ref, woa_ref, wob_ref,
                 o_ref, st_ref, kvs_ref, oa_ref, ob_ref):
    tile = x_ref.shape[1]
    seq_tile = pl.program_id(1)

    @pl.when(seq_tile == 0)
    def _():
        st_ref[...] = jnp.zeros_like(st_ref)
        kvs_ref[0:SWA_BLOCK, :] = jnp.zeros((SWA_BLOCK, 2 * PAIR_W), F32)

    x = x_ref[0]
    h = _rms_rows(x, gain_ref[...]).astype(BF16)
    proj = _dot(h, wmain_ref[...])
    gq = proj[:, 0 * MIX_W:1 * MIX_W]
    gk = proj[:, 1 * MIX_W:2 * MIX_W]
    gv = proj[:, 2 * MIX_W:3 * MIX_W]
    gr = proj[:, 3 * MIX_W:4 * MIX_W]
    sq = proj[:, 4 * MIX_W:5 * MIX_W]
    kv = _dot(h, wkv_ref[...])
    glr = _dot(h, wglr_ref[...])
    seg = seg_ref[...]

    gate = _dot(glr.astype(BF16), wgate_ref[...]) + bgate_ref[...]
    log_g = (jnp.minimum(gate, 0.0) - jnp.log1p(jnp.exp(-jnp.abs(gate)))) * (1.0 / GLA_GATE_NORM)
    cum = _dot01_lhs(tril_ref[...], log_g)
    q_pos = (gq * HEAD_DIM ** -0.5) * jnp.exp(cum)
    k_neg = gk * jnp.exp(-cum)
    row = lax.broadcasted_iota(jnp.int32, (CHUNK, PAIR_W), 0)
    lane = lax.broadcasted_iota(jnp.int32, (CHUNK, PAIR_W), 1)
    causal = row >= (lane % CHUNK)
    bd_row = lax.broadcasted_iota(jnp.int32, (PAIR_W, PAIR_W), 0)
    bd_lane = lax.broadcasted_iota(jnp.int32, (PAIR_W, PAIR_W), 1)
    same_head = (bd_row < HEAD_DIM) == (bd_lane < HEAD_DIM)
    chunks = range(tile // CHUNK)
    rows = [slice(c * CHUNK, (c + 1) * CHUNK) for c in chunks]
    lanes = [slice(p * PAIR_W, (p + 1) * PAIR_W) for p in range(PAIRS)]
    units = [(rs, ls) for rs in rows for ls in lanes]
    cum_last = [cum[rs][CHUNK - 1:CHUNK] for rs in rows]
    k_dec = jnp.concatenate([gk[rs] * jnp.exp(cl - cum[rs]) for rs, cl in zip(rows, cum_last)], axis=0)
    q_b = q_pos.astype(BF16)
    kn_b = k_neg.astype(BF16)
    kd_b = k_dec.astype(BF16)
    v_b = gv.astype(BF16)
    scores = [_dot_nt(q_b[rs, ls], _block_diag_pair(kn_b[rs, ls])) for rs, ls in units]
    scores = [jnp.where(causal, s, 0.0).astype(BF16) for s in scores]
    o_intra = [_dot(s, _block_diag_pair(v_b[rs, ls])) for s, (rs, ls) in zip(scores, units)]
    upd = [jnp.where(same_head, _dot_tn(v_b[rs, ls], kd_b[rs, ls]), 0.0) for rs, ls in units]
    states = []
    for p, ls in enumerate(lanes):
        state_t = st_ref[p]
        for c, cl in enumerate(cum_last):
            states.append(((c, p), state_t))
            state_t = state_t * jnp.exp(cl[:, ls]) + upd[c * PAIRS + p]
        st_ref[p] = state_t
    states = dict(states)
    for c, rs in enumerate(rows):
        for p, ls in enumerate(lanes):
            o_inter = _dot_nt(q_b[rs, ls], states[(c, p)].astype(BF16))
            oa_ref[rs, ls] = o_intra[c * PAIRS + p] + o_inter
    o_a = oa_ref[...]
    o_a = o_a * lax.rsqrt(_seg_mean(o_a * o_a, seg) + NORM_EPS) * glan_ref[...] * _silu(gr)

    qn = sq * lax.rsqrt(_seg_mean(sq * sq, seg) + NORM_EPS) * qn_ref[...]
    sk = kv[:, 0:PAIR_W]
    kn = sk * lax.rsqrt(_seg_mean(sk * sk, seg[0:PAIR_W, 0:PAIR_W]) + NORM_EPS) * kn_ref[...]
    kvs_ref[SWA_BLOCK:SWA_BLOCK + tile, 0:PAIR_W] = kn
    kvs_ref[SWA_BLOCK:SWA_BLOCK + tile, PAIR_W:2 * PAIR_W] = kv[:, PAIR_W:2 * PAIR_W]
    srow = lax.broadcasted_iota(jnp.int32, (2 * SWA_BLOCK, 2 * SWA_BLOCK), 0)
    skey = lax.broadcasted_iota(jnp.int32, (2 * SWA_BLOCK, 2 * SWA_BLOCK), 1)
    dist = (srow % SWA_BLOCK) + SWA_BLOCK - skey
    valid = (dist >= 0) & (dist < SWA_BLOCK)
    dist_f = dist.astype(F32)
    top = srow < SWA_BLOCK
    top_col = top[:, 0:1]
    ones_cols = jnp.ones((2 * SWA_BLOCK, PAIR_W), BF16)
    first_lane = lax.broadcasted_iota(jnp.int32, (2 * SWA_BLOCK, PAIR_W), 1) < HEAD_DIM
    no_prev = jnp.where(seq_tile > 0, 0.0, -jnp.inf)
    for j in range(tile // SWA_BLOCK):
        ks = slice(j * SWA_BLOCK, (j + 2) * SWA_BLOCK)
        keys = kvs_ref[ks, 0:PAIR_W].astype(BF16)
        vals = kvs_ref[ks, PAIR_W:2 * PAIR_W].astype(BF16)
        zero_v = jnp.zeros_like(vals)
        vals0 = jnp.concatenate([jnp.where(first_lane, vals, zero_v), ones_cols], axis=1)
        vals1 = jnp.concatenate([jnp.where(first_lane, zero_v, vals), ones_cols], axis=1)
        for p in range(PAIRS):
            qs = _block_diag_pair(qn[j * SWA_BLOCK:(j + 1) * SWA_BLOCK,
                                     p * PAIR_W:(p + 1) * PAIR_W].astype(BF16))
            slope = jnp.where(top, 2.0 ** -(p + 1), 2.0 ** -(p + 1 + PAIRS))
            s = _dot_nt(qs, keys) * HEAD_DIM ** -0.5 - slope * dist_f
            s = jnp.where(valid, s, -jnp.inf)
            if j == 0:
                s = s + jnp.where(skey < SWA_BLOCK, no_prev, 0.0)
            sink = jnp.where(top_col, sinks_ref[p], sinks_ref[p + PAIRS])
            m = jnp.maximum(jnp.max(s, axis=-1, keepdims=True), sink)
            pe = jnp.exp(s - m).astype(BF16)
            sink_e = jnp.exp(sink - m)
            r0 = _dot(pe[0:SWA_BLOCK], vals0)
            r1 = _dot(pe[SWA_BLOCK:], vals1)
            o_pair = (r0[:, 0:PAIR_W] / (r0[:, PAIR_W:] + sink_e[0:SWA_BLOCK])
                      + r1[:, 0:PAIR_W] / (r1[:, PAIR_W:] + sink_e[SWA_BLOCK:]))
            ob_ref[j * SWA_BLOCK:(j + 1) * SWA_BLOCK, p * PAIR_W:(p + 1) * PAIR_W] = o_pair
    kvs_ref[0:SWA_BLOCK, :] = kvs_ref[tile:tile + SWA_BLOCK, :]

    out = _dot(o_a.astype(BF16), woa_ref[...]) + _dot(ob_ref[...].astype(BF16), wob_ref[...])
    o_ref[0] = x + out


def _block_tril(tile):
    r = jnp.arange(tile)
    same = (r[:, None] // CHUNK) == (r[None, :] // CHUNK)
    return (same & (r[:, None] >= r[None, :])).astype(BF16)


def _seg_matrix():
    r = jnp.arange(MIX_W)
    return ((r[:, None] // HEAD_DIM) == (r[None, :] // HEAD_DIM)).astype(BF16)


def _head_tile(v, reps):
    return jnp.tile(v.astype(F32), reps).reshape(1, reps * v.shape[0])


def _even_layer(x, gain, w_in, w_gate, b_gate, gla_norm, q_norm, k_norm, sinks, w_out):
    bsz, seq, _ = x.shape
    tile = min(MIX_TILE, seq)
    order = jnp.array([h for p in range(PAIRS) for h in (p, p + PAIRS)])
    c0 = 4 * MIX_W
    glr0 = c0
    sq0 = c0 + GLA_RANK
    sk0 = sq0 + MIX_W
    w_sq = w_in[:, sq0:sk0].reshape(D_MODEL, HEADS, HEAD_DIM)[:, order].reshape(D_MODEL, MIX_W)
    w_main = jnp.concatenate([w_in[:, :c0], w_sq], axis=1).astype(BF16)
    w_kv = w_in[:, sk0:].astype(BF16)
    w_glr = jnp.pad(w_in[:, glr0:sq0], ((0, 0), (0, LANES - GLA_RANK))).astype(BF16)
    w_gate_p = jnp.pad(w_gate, ((0, LANES - GLA_RANK), (0, 0))).astype(BF16)
    w_out_a = w_out[:MIX_W].astype(BF16)
    w_out_b = w_out[MIX_W:].reshape(HEADS, HEAD_DIM, D_MODEL)[order].reshape(MIX_W, D_MODEL).astype(BF16)
    args = (x, gain.reshape(1, D_MODEL), w_main, w_kv, w_glr, w_gate_p, b_gate.reshape(1, MIX_W),
            _head_tile(gla_norm, HEADS), _head_tile(q_norm, HEADS), _head_tile(k_norm, SWA_KV_HEADS),
            _seg_matrix(), _block_tril(tile), w_out_a, w_out_b)
    in_specs = ([pl.BlockSpec(memory_space=pltpu.SMEM), _tile_spec(tile, D_MODEL)]
                + [_const_spec(a.shape) for a in args[1:]])
    return pl.pallas_call(
        _even_kernel,
        grid=(bsz, seq // tile),
        in_specs=in_specs,
        out_specs=_tile_spec(tile, D_MODEL),
        out_shape=jax.ShapeDtypeStruct(x.shape, F32),
        scratch_shapes=[pltpu.VMEM((PAIRS, PAIR_W, PAIR_W), F32),
                        pltpu.VMEM((SWA_BLOCK + tile, 2 * PAIR_W), F32),
                        pltpu.VMEM((tile, MIX_W), F32),
                        pltpu.VMEM((tile, MIX_W), F32)],
        compiler_params=_params(),
        name="gla_swa_mixer",
    )(sinks.astype(F32), *args)


def _s5_prep_kernel(are_ref, aim_ref, ldt_ref, bre_ref, bim_ref, pre_ref, pim_ref, bbre_ref, bbim_ref):
    a_re = are_ref[...]
    a_im = aim_ref[...]
    dt = jnp.exp(ldt_ref[...])
    for k in range(SCAN_LEVELS):
        step = float(2 ** k)
        mag = jnp.exp(step * dt * a_re)
        ang = step * dt * a_im
        pre_ref[k] = mag * jnp.cos(ang)
        pim_ref[k] = mag * jnp.sin(ang)
    abar_re = pre_ref[0]
    abar_im = pim_ref[0]
    den = a_re * a_re + a_im * a_im
    f_re = ((abar_re - 1.0) * a_re + abar_im * a_im) / den
    f_im = (abar_im * a_re - (abar_re - 1.0) * a_im) / den
    for c in range(S5_GROUP):
        bbre_ref[c] = f_re * bre_ref[c] - f_im * bim_ref[c]
        bbim_ref[c] = f_re * bim_ref[c] + f_im * bre_ref[c]


def _s5_kernel(x_ref, gain_ref, wsu_ref, bre_ref, bim_ref, pre_ref, pim_ref, cre_ref, cim_ref,
               dskip_ref, wglu_ref, bglu_ref, o_ref, buf_ref, st_ref, xs_ref):
    tile = x_ref.shape[1]
    half_in = MIX_W // 2
    half_state = S5_LANES // 2

    @pl.when(pl.program_id(1) == 0)
    def _():
        st_ref[...] = jnp.zeros_like(st_ref)
        buf_ref[:, :, :, 0:SCAN_PAD, :] = jnp.zeros((2, 2, S5_LANE_BLOCKS, SCAN_PAD, LANES), F32)

    x = x_ref[0]
    h = _rms_rows(x, gain_ref[...]).astype(BF16)
    u = _dot(h, wsu_ref[...])
    ub = u.astype(BF16)
    for kt in range(2):
        u_half = ub[:, kt * half_in:(kt + 1) * half_in]
        bu_re = _dot(u_half, bre_ref[kt])
        bu_im = _dot(u_half, bim_ref[kt])
        for j in range(S5_LANE_BLOCKS // 2):
            lb = kt * (S5_LANE_BLOCKS // 2) + j
            buf_ref[0, 0, lb, SCAN_PAD:SCAN_PAD + tile, :] = bu_re[:, j * LANES:(j + 1) * LANES]
            buf_ref[0, 1, lb, SCAN_PAD:SCAN_PAD + tile, :] = bu_im[:, j * LANES:(j + 1) * LANES]

    def scan_block(lb, carry):
        p_re = pre_ref[lb]
        p_im = pim_ref[lb]
        s_re = st_ref[0, lb, 0:1, :]
        s_im = st_ref[1, lb, 0:1, :]
        r0 = buf_ref[0, 0, lb, SCAN_PAD:SCAN_PAD + 1, :]
        i0 = buf_ref[0, 1, lb, SCAN_PAD:SCAN_PAD + 1, :]
        buf_ref[0, 0, lb, SCAN_PAD:SCAN_PAD + 1, :] = r0 + p_re[0:1] * s_re - p_im[0:1] * s_im
        buf_ref[0, 1, lb, SCAN_PAD:SCAN_PAD + 1, :] = i0 + p_re[0:1] * s_im + p_im[0:1] * s_re
        for k in range(SCAN_LEVELS):
            src, dst = k % 2, 1 - k % 2
            shift = 2 ** k
            cur_re = buf_ref[src, 0, lb, SCAN_PAD:SCAN_PAD + tile, :]
            cur_im = buf_ref[src, 1, lb, SCAN_PAD:SCAN_PAD + tile, :]
            sh_re = buf_ref[src, 0, lb, SCAN_PAD - shift:SCAN_PAD - shift + tile, :]
            sh_im = buf_ref[src, 1, lb, SCAN_PAD - shift:SCAN_PAD - shift + tile, :]
            a_re = p_re[k:k + 1]
            a_im = p_im[k:k + 1]
            buf_ref[dst, 0, lb, SCAN_PAD:SCAN_PAD + tile, :] = cur_re + a_re * sh_re - a_im * sh_im
            buf_ref[dst, 1, lb, SCAN_PAD:SCAN_PAD + tile, :] = cur_im + a_re * sh_im + a_im * sh_re
        fin = SCAN_LEVELS % 2
        x_re = buf_ref[fin, 0, lb, SCAN_PAD:SCAN_PAD + tile, :]
        x_im = buf_ref[fin, 1, lb, SCAN_PAD:SCAN_PAD + tile, :]
        st_ref[0, lb, 0:1, :] = x_re[tile - 1:tile]
        st_ref[1, lb, 0:1, :] = x_im[tile - 1:tile]
        xs_ref[0, lb] = x_re.astype(BF16)
        xs_ref[1, lb] = x_im.astype(BF16)
        return carry

    lax.fori_loop(0, S5_LANE_BLOCKS, scan_block, 0)

    ys = []
    for kt in range(2):
        blocks = range(kt * (S5_LANE_BLOCKS // 2), (kt + 1) * (S5_LANE_BLOCKS // 2))
        x_re = jnp.concatenate([xs_ref[0, lb] for lb in blocks], axis=1)
        x_im = jnp.concatenate([xs_ref[1, lb] for lb in blocks], axis=1)
        ys.append(_dot(x_re, cre_ref[kt]) - _dot(x_im, cim_ref[kt]))
    y = jnp.concatenate(ys, axis=1) + dskip_ref[...] * u
    y = 0.5 * y * (1.0 + jnp.tanh(math.sqrt(2.0 / math.pi) * (y + 0.044715 * (y * y * y))))
    gate = _sigmoid(_dot(y.astype(BF16), wglu_ref[...]) + bglu_ref[...])
    o_ref[0] = y * gate


def _s5_branch(x, gain, w_su, a_re, a_im, log_dt, b_re, b_im, c_re, c_im, d_skip, w_glu, b_glu):
    bsz, seq, _ = x.shape
    tile = min(MIX_TILE, seq)
    assert tile == MIX_TILE, "S5 scan levels are sized for MIX_TILE rows"
    gs = (S5_GROUPS, S5_STATE)
    pow_re, pow_im, bb_re, bb_im = pl.pallas_call(
        _s5_prep_kernel,
        out_shape=[jax.ShapeDtypeStruct((SCAN_LEVELS,) + gs, F32)] * 2
        + [jax.ShapeDtypeStruct((S5_GROUP,) + gs, F32)] * 2,
        name="s5_discretize",
    )(a_re, a_im, log_dt.reshape(S5_GROUPS, 1), b_re.transpose(2, 0, 1), b_im.transpose(2, 0, 1))
    half_g = S5_GROUPS // 2
    eye = jnp.eye(half_g, dtype=F32)

    def in_blocks(bb):
        t = bb.reshape(S5_GROUP, 2, half_g, S5_STATE)
        return jnp.einsum('ckgp,gh->kgchp', t, eye).reshape(2, half_g * S5_GROUP, half_g * S5_STATE).astype(BF16)

    def out_blocks(cc):
        t = cc.reshape(2, half_g, S5_GROUP, S5_STATE)
        return jnp.einsum('kgop,gh->kgpho', t, eye).reshape(2, half_g * S5_STATE, half_g * S5_GROUP).astype(BF16)

    def lane_blocks(pw):
        return pw.reshape(SCAN_LEVELS, S5_LANE_BLOCKS, LANES).transpose(1, 0, 2)

    args = (x, gain.reshape(1, D_MODEL), w_su.astype(BF16), in_blocks(bb_re), in_blocks(bb_im),
            lane_blocks(pow_re), lane_blocks(pow_im), out_blocks(c_re), out_blocks(c_im),
            d_skip.reshape(1, MIX_W), w_glu.astype(BF16), b_glu.reshape(1, MIX_W))
    in_specs = [_tile_spec(tile, D_MODEL)] + [_const_spec(a.shape) for a in args[1:]]
    return pl.pallas_call(
        _s5_kernel,
        grid=(bsz, seq // tile),
        in_specs=in_specs,
        out_specs=_tile_spec(tile, MIX_W),
        out_shape=jax.ShapeDtypeStruct((bsz, seq, MIX_W), F32),
        scratch_shapes=[pltpu.VMEM((2, 2, S5_LANE_BLOCKS, SCAN_PAD + tile, LANES), F32),
                        pltpu.VMEM((2, S5_LANE_BLOCKS, SUBLANES, LANES), F32),
                        pltpu.VMEM((2, S5_LANE_BLOCKS, tile, LANES), BF16)],
        compiler_params=_params(),
        name="s5_branch",
    )(*args)


def _pair_matmul(a, b):
    return _dot(a.astype(BF16), _block_diag_pair(b.astype(BF16)))


def _gdn_kernel(x_ref, oc_ref, gain_ref, wqkvz_ref, wda_ref, wdb_ref, cw_ref, alog_ref, dtb_ref,
                gnorm_ref, seg_ref, tril_ref, onesbd_ref, woc_ref, wod_ref,
                o_ref, st_ref, cs_ref, od_ref, attn_ref, u_ref, wq_ref):
    tile = x_ref.shape[1]
    qkv_w = 3 * MIX_W

    @pl.when(pl.program_id(1) == 0)
    def _():
        st_ref[...] = jnp.zeros_like(st_ref)
        cs_ref[0:SUBLANES, :] = jnp.zeros((SUBLANES, qkv_w), F32)

    x = x_ref[0]
    h = _rms_rows(x, gain_ref[...]).astype(BF16)
    proj = _dot(h, wqkvz_ref[...])
    z = proj[:, qkv_w:]
    da = _dot(h, wda_ref[...])
    db = _dot(h, wdb_ref[...])
    seg = seg_ref[...]

    cs_ref[SUBLANES:SUBLANES + tile, :] = proj[:, :qkv_w]
    conv = jnp.zeros((tile, qkv_w), F32)
    for k in range(GDN_CONV):
        off = SUBLANES - (GDN_CONV - 1) + k
        conv = conv + cw_ref[k:k + 1, :] * cs_ref[off:off + tile, :]
    cs_ref[0:SUBLANES, :] = cs_ref[tile:tile + SUBLANES, :]
    qkv = _silu(conv)
    q = qkv[:, 0:MIX_W]
    k_ = qkv[:, MIX_W:2 * MIX_W]
    v = qkv[:, 2 * MIX_W:]
    q = q * lax.rsqrt(_seg_mean(q * q, seg) * HEAD_DIM + NORM_EPS) * HEAD_DIM ** -0.5
    k_ = k_ * lax.rsqrt(_seg_mean(k_ * k_, seg) * HEAD_DIM + NORM_EPS)

    log_alpha = -jnp.exp(alog_ref[...]) * _softplus(da + dtb_ref[...])
    beta = _sigmoid(db)
    g = _dot01_lhs(tril_ref[...], log_alpha)
    row = lax.broadcasted_iota(jnp.int32, (tile, MIX_W), 0) % CHUNK
    key = lax.broadcasted_iota(jnp.int32, (tile, MIX_W), 1) % CHUNK
    g_key = _dot01_lhs(onesbd_ref[...], jnp.where(row <= key, log_alpha, 0.0))
    decay_incl = jnp.exp(jnp.where(row >= key, g - g_key, -jnp.inf))
    decay_strict = jnp.where(row > key, decay_incl, 0.0)
    exp_g = jnp.exp(g)
    v_beta = v * beta
    k_beta_g = k_ * beta * exp_g
    q_dec = q * exp_g

    eye_pair = (lax.broadcasted_iota(jnp.int32, (CHUNK, PAIR_W), 0)
                == lax.broadcasted_iota(jnp.int32, (CHUNK, PAIR_W), 1) % CHUNK).astype(F32)
    bd_row = lax.broadcasted_iota(jnp.int32, (PAIR_W, PAIR_W), 0)
    bd_lane = lax.broadcasted_iota(jnp.int32, (PAIR_W, PAIR_W), 1)
    same_head = (bd_row < HEAD_DIM) == (bd_lane < HEAD_DIM)
    first_lane = lax.broadcasted_iota(jnp.int32, (CHUNK, PAIR_W), 1) < HEAD_DIM
    rows = [slice(c * CHUNK, (c + 1) * CHUNK) for c in range(tile // CHUNK)]
    lanes = [slice(p * PAIR_W, (p + 1) * PAIR_W) for p in range(PAIRS)]
    units = [(rs, ls) for rs in rows for ls in lanes]
    g_last = [g[rs][CHUNK - 1:CHUNK] for rs in rows]
    k_dec = jnp.concatenate([k_[rs] * jnp.exp(gl - g[rs]) for rs, gl in zip(rows, g_last)], axis=0)
    k_b = k_.astype(BF16)
    q_b = q.astype(BF16)
    kd_b = k_dec.astype(BF16)
    sc = [_dot_nt(jnp.concatenate([k_b[rs, ls], q_b[rs, ls]], axis=0), _block_diag_pair(k_b[rs, ls]))
          for rs, ls in units]
    power = [beta[rs, ls] * s[0:CHUNK] * decay_strict[rs, ls] for s, (rs, ls) in zip(sc, units)]
    for s, (rs, ls) in zip(sc, units):
        attn_ref[rs, ls] = (s[CHUNK:] * decay_incl[rs, ls]).astype(BF16)
    inv = [eye_pair - low for low in power]
    for _ in range(int(math.log2(CHUNK)) - 1):
        power = [_pair_matmul(pw, pw) for pw in power]
        inv = [iv + _pair_matmul(iv, pw) for iv, pw in zip(inv, power)]
    vb_b = v_beta.astype(BF16)
    kbg_b = k_beta_g.astype(BF16)
    zero = jnp.zeros((CHUNK, PAIR_W), BF16)
    for iv, (rs, ls) in zip(inv, units):
        vb = vb_b[rs, ls]
        kbg = kbg_b[rs, ls]
        rhs = jnp.concatenate(
            [jnp.concatenate([jnp.where(first_lane, vb, zero), jnp.where(first_lane, kbg, zero)], axis=1),
             jnp.concatenate([jnp.where(first_lane, zero, vb), jnp.where(first_lane, zero, kbg)], axis=1)],
            axis=0)
        uw = _dot(iv.astype(BF16), rhs)
        u_ref[rs, ls] = uw[:, 0:PAIR_W]
        wq_ref[rs, ls] = uw[:, PAIR_W:].astype(BF16)
    qd_b = q_dec.astype(BF16)
    for rs, gl in zip(rows, g_last):
        state = [st_ref[p] for p in range(PAIRS)]
        ws_qs = [_dot(jnp.concatenate([wq_ref[rs, ls], qd_b[rs, ls]], axis=0), st.astype(BF16))
                 for ls, st in zip(lanes, state)]
        v_new = [(u_ref[rs, ls] - wq[0:CHUNK]).astype(BF16) for ls, wq in zip(lanes, ws_qs)]
        upd = [_dot_tn(kd_b[rs, ls], vn) for ls, vn in zip(lanes, v_new)]
        for p, ls in enumerate(lanes):
            st_ref[p] = state[p] * jnp.exp(gl[:, ls]) + jnp.where(same_head, upd[p], 0.0)
        for ls, wq, vn in zip(lanes, ws_qs, v_new):
            od_ref[rs, ls] = wq[CHUNK:] + _dot(attn_ref[rs, ls], _block_diag_pair(vn))
    o_d = od_ref[...]
    o_d = o_d * lax.rsqrt(_seg_mean(o_d * o_d, seg) + NORM_EPS) * gnorm_ref[...] * _silu(z)
    out = _dot(oc_ref[0].astype(BF16), woc_ref[...]) + _dot(o_d.astype(BF16), wod_ref[...])
    o_ref[0] = x + out


def _block_ones(tile):
    r = jnp.arange(tile)
    return ((r[:, None] // CHUNK) == (r[None, :] // CHUNK)).astype(BF16)


def _gdn_layer(x, o_c, gain, w_qkvz, w_da, w_db, conv_w, a_log, dt_bias, gdn_norm, w_out):
    bsz, seq, _ = x.shape
    tile = min(MIX_TILE, seq)
    rep = lambda w: jnp.repeat(w, HEAD_DIM, axis=-1)
    args = (x, o_c, gain.reshape(1, D_MODEL), w_qkvz.astype(BF16), rep(w_da).astype(BF16),
            rep(w_db).astype(BF16), conv_w, rep(a_log).reshape(1, MIX_W), rep(dt_bias).reshape(1, MIX_W),
            _head_tile(gdn_norm, HEADS), _seg_matrix(), _block_tril(tile), _block_ones(tile),
            w_out[:MIX_W].astype(BF16), w_out[MIX_W:].astype(BF16))
    in_specs = ([_tile_spec(tile, D_MODEL), _tile_spec(tile, MIX_W)]
                + [_const_spec(a.shape) for a in args[2:]])
    return pl.pallas_call(
        _gdn_kernel,
        grid=(bsz, seq // tile),
        in_specs=in_specs,
        out_specs=_tile_spec(tile, D_MODEL),
        out_shape=jax.ShapeDtypeStruct(x.shape, F32),
        scratch_shapes=[pltpu.VMEM((PAIRS, PAIR_W, PAIR_W), F32),
                        pltpu.VMEM((SUBLANES + tile, 3 * MIX_W), F32),
                        pltpu.VMEM((tile, MIX_W), F32),
                        pltpu.VMEM((tile, MIX_W), BF16),
                        pltpu.VMEM((tile, MIX_W), F32),
                        pltpu.VMEM((tile, MIX_W), BF16)],
        compiler_params=_params(),
        name="gdn_mixer",
    )(*args)


def _odd_layer(x, gain, w_in, a_re, a_im, log_dt, b_re, b_im, c_re, c_im, d_skip, w_glu, b_glu,
               conv_w, a_log, dt_bias, gdn_norm, w_out):
    su1 = MIX_W
    z1 = su1 + 4 * MIX_W
    o_c = _s5_branch(x, gain, w_in[:, :su1], a_re, a_im, log_dt, b_re, b_im, c_re, c_im,
                     d_skip, w_glu, b_glu)
    return _gdn_layer(x, o_c, gain, w_in[:, su1:z1], w_in[:, z1:z1 + HEADS], w_in[:, z1 + HEADS:],
                      conv_w, a_log, dt_bias, gdn_norm, w_out)


def kernel(x, norm_mix, norm_ffn, w_in_even, w_gla_gate, b_gla_gate, gla_out_norm, swa_q_norm, swa_k_norm, swa_sinks, w_out_even, w_in_odd, s5_a_re, s5_a_im, s5_log_dt, s5_b_re, s5_b_im, s5_c_re, s5_c_im, s5_d, s5_w_glu, s5_b_glu, gdn_conv_w, gdn_a_log, gdn_dt_bias, gdn_out_norm, w_out_odd, w_ffn_up, ffn_conv_w, ffn_conv_b, w_ffn_down):
    depth = norm_mix.shape[0]
    for layer in range(depth):
        i = layer // 2
        if layer % 2 == 0:
            x = _even_layer(x, norm_mix[layer], w_in_even[i], w_gla_gate[i], b_gla_gate[i],
                            gla_out_norm[i], swa_q_norm[i], swa_k_norm[i], swa_sinks[i], w_out_even[i])
        else:
            x = _odd_layer(x, norm_mix[layer], w_in_odd[i], s5_a_re[i], s5_a_im[i], s5_log_dt[i],
                           s5_b_re[i], s5_b_im[i], s5_c_re[i], s5_c_im[i], s5_d[i], s5_w_glu[i],
                           s5_b_glu[i], gdn_conv_w[i], gdn_a_log[i], gdn_dt_bias[i], gdn_out_norm[i],
                           w_out_odd[i])
        x = _ffn_layer(x, norm_ffn[layer], w_ffn_up[layer], ffn_conv_w[layer], ffn_conv_b[layer],
                       w_ffn_down[layer])
    return x
```

```python
import math

import jax
import jax.numpy as jnp
from jax import lax
from jax.experimental import pallas as pl
from jax.experimental.pallas import tpu as pltpu

F32 = jnp.float32
BF16 = jnp.bfloat16

D_MODEL = 1024
HEAD_DIM = 64
HEADS = 8
MIX_W = HEADS * HEAD_DIM
PAIRS = HEADS // 2
PAIR_W = 2 * HEAD_DIM
GLA_RANK = 16
GLA_GATE_NORM = 16.0
CHUNK = 64
SWA_BLOCK = 128
SWA_KV_HEADS = 2
S5_GROUP = 16
S5_GROUPS = 32
S5_STATE = 64
S5_LANES = S5_GROUPS * S5_STATE
GDN_CONV = 4
FFN_DIM = 2816
FFN_CONV = 3
NORM_EPS = 1e-6
LANES = 128
SUBLANES = 8
MXU_COLS = 256
VMEM_LIMIT = 56 * 1024 * 1024
S5_LANE_BLOCKS = S5_LANES // LANES

MIX_TILE = 256
FFN_TILE = 512
FFN_COLS = MXU_COLS
SCAN_LEVELS = int(math.log2(SUBLANES)) + 1
SCAN_UNROLL = 4
MXU = "mxu"
VPU = "vpu"


def _dot(a, b):
    return jnp.dot(a, b, preferred_element_type=F32)


def _dot_nt(a, b):
    return lax.dot_general(a, b, (((1,), (1,)), ((), ())), preferred_element_type=F32)


def _dot_tn(a, b):
    return lax.dot_general(a, b, (((0,), (0,)), ((), ())), preferred_element_type=F32)


def _split3(x):
    hi = x.astype(BF16)
    r1 = x - hi.astype(F32)
    mid = r1.astype(BF16)
    lo = (r1 - mid.astype(F32)).astype(BF16)
    return hi, mid, lo


def _dot01_lhs(m01, x):
    hi, mid, lo = _split3(x)
    return _dot(m01, hi) + _dot(m01, mid) + _dot(m01, lo)


def _sigmoid(x):
    return 1.0 / (1.0 + jnp.exp(-x))


def _silu(x):
    return x * _sigmoid(x)


def _softplus(x):
    return jnp.maximum(x, 0.0) + jnp.log1p(jnp.exp(-jnp.abs(x)))


def _rms_rows(x, gain_row):
    ms = jnp.mean(x * x, axis=-1, keepdims=True)
    return x * lax.rsqrt(ms + NORM_EPS) * gain_row


def _seg_mean(y, seg01):
    return _dot(y.astype(BF16), seg01) * (1.0 / HEAD_DIM)


def _block_diag_pair(v):
    first = lax.broadcasted_iota(jnp.int32, v.shape, 1) < HEAD_DIM
    zero = jnp.zeros_like(v)
    return jnp.concatenate([jnp.where(first, v, zero), jnp.where(first, zero, v)], axis=0)


def _pair_matmul(a, b):
    return _dot(a.astype(BF16), _block_diag_pair(b.astype(BF16)))


def _same_head_mask():
    r = lax.broadcasted_iota(jnp.int32, (PAIR_W, PAIR_W), 0)
    c = lax.broadcasted_iota(jnp.int32, (PAIR_W, PAIR_W), 1)
    return (r < HEAD_DIM) == (c < HEAD_DIM)


def _causal_conv(u_ref, j, taps_ref, lanes, tile, acc=None):
    taps = taps_ref.shape[0]
    for k in range(taps):
        off = SUBLANES - (taps - 1) + k
        term = taps_ref[k:k + 1, lanes] * u_ref[j, off:off + tile, :]
        acc = term if acc is None else acc + term
    return acc


def _interleave(*streams, alternate=True):
    results = [None] * len(streams)

    def advance(i):
        try:
            return next(streams[i])
        except StopIteration as stop:
            results[i] = stop.value
            return None

    tags = [advance(i) for i in range(len(streams))]
    live = [i for i, t in enumerate(tags) if t is not None]
    last, turn = None, 0
    while live:
        choice = ([i for i in live if tags[i] != last] or live) if alternate else live[:1]
        i = choice[turn % len(choice)]
        turn += 1
        last = tags[i]
        tags[i] = advance(i)
        if tags[i] is None:
            live.remove(i)
    return results


def _const_spec(shape):
    nd = len(shape)
    return pl.BlockSpec(shape, lambda b, l: (0,) * nd, pipeline_mode=pl.Buffered(1))


def _tile_spec(tile, width):
    return pl.BlockSpec((1, tile, width), lambda b, l: (b, l, 0))


def _params():
    return pltpu.CompilerParams(dimension_semantics=("arbitrary", "arbitrary"),
                                vmem_limit_bytes=VMEM_LIMIT)


def _ffn_kernel(x_ref, gain_ref, wa_ref, wg_ref, cwa_ref, cwg_ref, cba_ref, cbg_ref, wd_ref,
                o_ref, ua_ref, ug_ref, act_ref):
    tile = x_ref.shape[1]
    blocks = FFN_COLS // LANES

    @pl.when(pl.program_id(1) == 0)
    def _():
        ua_ref[:, 0:SUBLANES, :] = jnp.zeros((FFN_DIM // LANES, SUBLANES, LANES), F32)
        ug_ref[:, 0:SUBLANES, :] = jnp.zeros((FFN_DIM // LANES, SUBLANES, LANES), F32)

    x = x_ref[0]
    h = _rms_rows(x, gain_ref[...]).astype(BF16)
    for c in range(FFN_DIM // FFN_COLS):
        sl = slice(c * FFN_COLS, (c + 1) * FFN_COLS)
        ua = _dot(h, wa_ref[:, sl])
        ug = _dot(h, wg_ref[:, sl])
        for t in range(blocks):
            j = c * blocks + t
            lanes = slice(j * LANES, (j + 1) * LANES)
            ua_ref[j, SUBLANES:SUBLANES + tile, :] = ua[:, t * LANES:(t + 1) * LANES]
            ug_ref[j, SUBLANES:SUBLANES + tile, :] = ug[:, t * LANES:(t + 1) * LANES]
            a = _causal_conv(ua_ref, j, cwa_ref, lanes, tile, cba_ref[:, lanes])
            g = _causal_conv(ug_ref, j, cwg_ref, lanes, tile, cbg_ref[:, lanes])
            act_ref[:, lanes] = (_silu(g) * a).astype(BF16)
    ua_ref[:, 0:SUBLANES, :] = ua_ref[:, tile:tile + SUBLANES, :]
    ug_ref[:, 0:SUBLANES, :] = ug_ref[:, tile:tile + SUBLANES, :]
    o_ref[0] = x + _dot(act_ref[...], wd_ref[...])


def _ffn_layer(x, gain, w_up, conv_w, conv_b, w_down):
    bsz, seq, _ = x.shape
    tile = min(FFN_TILE, seq)
    wa = w_up[:, :FFN_DIM].astype(BF16)
    wg = w_up[:, FFN_DIM:].astype(BF16)
    args = (x, gain.reshape(1, D_MODEL), wa, wg, conv_w[:, :FFN_DIM], conv_w[:, FFN_DIM:],
            conv_b[:FFN_DIM].reshape(1, FFN_DIM), conv_b[FFN_DIM:].reshape(1, FFN_DIM),
            w_down.astype(BF16))
    in_specs = [_tile_spec(tile, D_MODEL)] + [_const_spec(a.shape) for a in args[1:]]
    return pl.pallas_call(
        _ffn_kernel,
        grid=(bsz, seq // tile),
        in_specs=in_specs,
        out_specs=_tile_spec(tile, D_MODEL),
        out_shape=jax.ShapeDtypeStruct(x.shape, F32),
        scratch_shapes=[pltpu.VMEM((FFN_DIM // LANES, SUBLANES + tile, LANES), F32),
                        pltpu.VMEM((FFN_DIM // LANES, SUBLANES + tile, LANES), F32),
                        pltpu.VMEM((tile, FFN_DIM), BF16)],
        compiler_params=_params(),
        name="conv_ffn",
    )(*args)


def _gla_steps(h, tile, wmain_ref, wglr_ref, wgate_ref, bgate_ref, glan_ref, seg_ref, tril_ref,
               st_ref, oa_ref):
    yield MXU
    proj = _dot(h, wmain_ref[:, 0:4 * MIX_W])
    gq = proj[:, 0 * MIX_W:1 * MIX_W]
    gk = proj[:, 1 * MIX_W:2 * MIX_W]
    gv = proj[:, 2 * MIX_W:3 * MIX_W]
    gr = proj[:, 3 * MIX_W:4 * MIX_W]
    glr = _dot(h, wglr_ref[...])
    gate = _dot(glr.astype(BF16), wgate_ref[...]) + bgate_ref[...]
    yield VPU
    log_g = (jnp.minimum(gate, 0.0) - jnp.log1p(jnp.exp(-jnp.abs(gate)))) * (1.0 / GLA_GATE_NORM)
    yield MXU
    cum = _dot01_lhs(tril_ref[...], log_g)
    yield VPU
    q_pos = (gq * HEAD_DIM ** -0.5) * jnp.exp(cum)
    k_neg = gk * jnp.exp(-cum)
    row = lax.broadcasted_iota(jnp.int32, (CHUNK, PAIR_W), 0)
    lane = lax.broadcasted_iota(jnp.int32, (CHUNK, PAIR_W), 1)
    causal = row >= (lane % CHUNK)
    same_head = _same_head_mask()
    rows = [slice(c * CHUNK, (c + 1) * CHUNK) for c in range(tile // CHUNK)]
    lanes = [slice(p * PAIR_W, (p + 1) * PAIR_W) for p in range(PAIRS)]
    cum_last = [cum[rs][CHUNK - 1:CHUNK] for rs in rows]
    k_dec = jnp.concatenate([gk[rs] * jnp.exp(cl - cum[rs]) for rs, cl in zip(rows, cum_last)], axis=0)
    q_b = q_pos.astype(BF16)
    kn_b = k_neg.astype(BF16)
    kd_b = k_dec.astype(BF16)
    v_b = gv.astype(BF16)
    scores, o_intra, upd = {}, {}, {}
    for c, rs in enumerate(rows):
        yield MXU
        for p, ls in enumerate(lanes):
            s = _dot_nt(q_b[rs, ls], _block_diag_pair(kn_b[rs, ls]))
            scores[c, p] = jnp.where(causal, s, 0.0).astype(BF16)
    for c, rs in enumerate(rows):
        yield MXU
        for p, ls in enumerate(lanes):
            upd[c, p] = jnp.where(same_head, _dot_tn(v_b[rs, ls], kd_b[rs, ls]), 0.0)
    for c, rs in enumerate(rows):
        yield MXU
        for p, ls in enumerate(lanes):
            o_intra[c, p] = _dot(scores[c, p], _block_diag_pair(v_b[rs, ls]))
    yield VPU
    states = {}
    for p, ls in enumerate(lanes):
        state_t = st_ref[p]
        for c, cl in enumerate(cum_last):
            states[c, p] = state_t
            state_t = state_t * jnp.exp(cl[:, ls]) + upd[c, p]
        st_ref[p] = state_t
    for c, rs in enumerate(rows):
        yield MXU
        for p, ls in enumerate(lanes):
            oa_ref[rs, ls] = o_intra[c, p] + _dot_nt(q_b[rs, ls], states[c, p].astype(BF16))
    yield MXU
    o_a = oa_ref[...]
    ms = _seg_mean(o_a * o_a, seg_ref[...])
    yield VPU
    return (o_a * lax.rsqrt(ms + NORM_EPS) * glan_ref[...] * _silu(gr)).astype(BF16)


def _swa_steps(h, tile, seq_tile, sinks_ref, wmain_ref, wkv_ref, qn_ref, kn_ref, seg_ref,
               kvs_ref, ob_ref):
    yield MXU
    sq = _dot(h, wmain_ref[:, 4 * MIX_W:5 * MIX_W])
    kv = _dot(h, wkv_ref[...])
    seg = seg_ref[...]
    yield MXU
    ms_q = _seg_mean(sq * sq, seg)
    sk = kv[:, 0:PAIR_W]
    ms_k = _seg_mean(sk * sk, seg[0:PAIR_W, 0:PAIR_W])
    yield VPU
    qn = (sq * lax.rsqrt(ms_q + NORM_EPS) * qn_ref[...]).astype(BF16)
    kvs_ref[SWA_BLOCK:SWA_BLOCK + tile, 0:PAIR_W] = sk * lax.rsqrt(ms_k + NORM_EPS) * kn_ref[...]
    kvs_ref[SWA_BLOCK:SWA_BLOCK + tile, PAIR_W:2 * PAIR_W] = kv[:, PAIR_W:2 * PAIR_W]
    srow = lax.broadcasted_iota(jnp.int32, (2 * SWA_BLOCK, 2 * SWA_BLOCK), 0)
    skey = lax.broadcasted_iota(jnp.int32, (2 * SWA_BLOCK, 2 * SWA_BLOCK), 1)
    dist = (srow % SWA_BLOCK) + SWA_BLOCK - skey
    valid = (dist >= 0) & (dist < SWA_BLOCK)
    dist_f = dist.astype(F32)
    top = srow < SWA_BLOCK
    top_col = top[:, 0:1]
    ones_cols = jnp.ones((2 * SWA_BLOCK, PAIR_W), BF16)
    first_lane = lax.broadcasted_iota(jnp.int32, (2 * SWA_BLOCK, PAIR_W), 1) < HEAD_DIM
    no_prev = jnp.where(seq_tile > 0, 0.0, -jnp.inf)
    for j in range(tile // SWA_BLOCK):
        ks = slice(j * SWA_BLOCK, (j + 2) * SWA_BLOCK)
        keys = kvs_ref[ks, 0:PAIR_W].astype(BF16)
        vals = kvs_ref[ks, PAIR_W:2 * PAIR_W].astype(BF16)
        zero_v = jnp.zeros_like(vals)
        vals0 = jnp.concatenate([jnp.where(first_lane, vals, zero_v), ones_cols], axis=1)
        vals1 = jnp.concatenate([jnp.where(first_lane, zero_v, vals), ones_cols], axis=1)
        for p in range(PAIRS):
            yield MXU
            qs = _block_diag_pair(qn[j * SWA_BLOCK:(j + 1) * SWA_BLOCK, p * PAIR_W:(p + 1) * PAIR_W])
            s = _dot_nt(qs, keys)
            yield VPU
            slope = jnp.where(top, 2.0 ** -(p + 1), 2.0 ** -(p + 1 + PAIRS))
            s = s * HEAD_DIM ** -0.5 - slope * dist_f
            s = jnp.where(valid, s, -jnp.inf)
            if j == 0:
                s = s + jnp.where(skey < SWA_BLOCK, no_prev, 0.0)
            sink = jnp.where(top_col, sinks_ref[p], sinks_ref[p + PAIRS])
            m = jnp.maximum(jnp.max(s, axis=-1, keepdims=True), sink)
            pe = jnp.exp(s - m).astype(BF16)
            sink_e = jnp.exp(sink - m)
            yield MXU
            r0 = _dot(pe[0:SWA_BLOCK], vals0)
            r1 = _dot(pe[SWA_BLOCK:], vals1)
            o_pair = (r0[:, 0:PAIR_W] / (r0[:, PAIR_W:] + sink_e[0:SWA_BLOCK])
                      + r1[:, 0:PAIR_W] / (r1[:, PAIR_W:] + sink_e[SWA_BLOCK:]))
            ob_ref[j * SWA_BLOCK:(j + 1) * SWA_BLOCK, p * PAIR_W:(p + 1) * PAIR_W] = o_pair.astype(BF16)
    kvs_ref[0:SWA_BLOCK, :] = kvs_ref[tile:tile + SWA_BLOCK, :]
    return ob_ref[...]


def _even_kernel(sinks_ref, x_ref, gain_ref, wmain_ref, wkv_ref, wglr_ref, wgate_ref, bgate_ref,
                 glan_ref, qn_ref, kn_ref, seg_ref, tril_ref, woa_ref, wob_ref,
                 o_ref, st_ref, kvs_ref, oa_ref, ob_ref):
    tile = x_ref.shape[1]
    seq_tile = pl.program_id(1)

    @pl.when(seq_tile == 0)
    def _():
        st_ref[...] = jnp.zeros_like(st_ref)
        kvs_ref[0:SWA_BLOCK, :] = jnp.zeros((SWA_BLOCK, 2 * PAIR_W), F32)

    x = x_ref[0]
    h = _rms_rows(x, gain_ref[...]).astype(BF16)
    o_a, o_b = _interleave(
        _gla_steps(h, tile, wmain_ref, wglr_ref, wgate_ref, bgate_ref, glan_ref, seg_ref, tril_ref,
                   st_ref, oa_ref),
        _swa_steps(h, tile, seq_tile, sinks_ref, wmain_ref, wkv_ref, qn_ref, kn_ref, seg_ref,
                   kvs_ref, ob_ref),
        alternate=False)
    o_ref[0] = x + _dot(o_a, woa_ref[...]) + _dot(o_b, wob_ref[...])


def _block_tril(tile):
    r = jnp.arange(tile)
    same = (r[:, None] // CHUNK) == (r[None, :] // CHUNK)
    return (same & (r[:, None] >= r[None, :])).astype(BF16)


def _block_ones(tile):
    r = jnp.arange(tile)
    return ((r[:, None] // CHUNK) == (r[None, :] // CHUNK)).astype(BF16)


def _seg_matrix():
    r = jnp.arange(MIX_W)
    return ((r[:, None] // HEAD_DIM) == (r[None, :] // HEAD_DIM)).astype(BF16)


def _head_tile(v, reps):
    return jnp.tile(v.astype(F32), reps).reshape(1, reps * v.shape[0])


def _even_layer(x, gain, w_in, w_gate, b_gate, gla_norm, q_norm, k_norm, sinks, w_out):
    bsz, seq, _ = x.shape
    tile = min(2 * MIX_TILE, seq)
    order = jnp.array([h for p in range(PAIRS) for h in (p, p + PAIRS)])
    c0 = 4 * MIX_W
    glr0 = c0
    sq0 = c0 + GLA_RANK
    sk0 = sq0 + MIX_W
    w_sq = w_in[:, sq0:sk0].reshape(D_MODEL, HEADS, HEAD_DIM)[:, order].reshape(D_MODEL, MIX_W)
    w_main = jnp.concatenate([w_in[:, :c0], w_sq], axis=1).astype(BF16)
    w_kv = w_in[:, sk0:].astype(BF16)
    w_glr = jnp.pad(w_in[:, glr0:sq0], ((0, 0), (0, LANES - GLA_RANK))).astype(BF16)
    w_gate_p = jnp.pad(w_gate, ((0, LANES - GLA_RANK), (0, 0))).astype(BF16)
    w_out_a = w_out[:MIX_W].astype(BF16)
    w_out_b = w_out[MIX_W:].reshape(HEADS, HEAD_DIM, D_MODEL)[order].reshape(MIX_W, D_MODEL).astype(BF16)
    args = (x, gain.reshape(1, D_MODEL), w_main, w_kv, w_glr, w_gate_p, b_gate.reshape(1, MIX_W),
            _head_tile(gla_norm, HEADS), _head_tile(q_norm, HEADS), _head_tile(k_norm, SWA_KV_HEADS),
            _seg_matrix(), _block_tril(tile), w_out_a, w_out_b)
    in_specs = ([pl.BlockSpec(memory_space=pltpu.SMEM), _tile_spec(tile, D_MODEL)]
                + [_const_spec(a.shape) for a in args[1:]])
    return pl.pallas_call(
        _even_kernel,
        grid=(bsz, seq // tile),
        in_specs=in_specs,
        out_specs=_tile_spec(tile, D_MODEL),
        out_shape=jax.ShapeDtypeStruct(x.shape, F32),
        scratch_shapes=[pltpu.VMEM((PAIRS, PAIR_W, PAIR_W), F32),
                        pltpu.VMEM((SWA_BLOCK + tile, 2 * PAIR_W), F32),
                        pltpu.VMEM((tile, MIX_W), F32),
                        pltpu.VMEM((tile, MIX_W), BF16)],
        compiler_params=_params(),
        name="gla_swa_mixer",
    )(sinks.astype(F32), *args)


def _s5_prep_kernel(are_ref, aim_ref, ldt_ref, bre_ref, bim_ref, pre_ref, pim_ref, bbre_ref, bbim_ref):
    a_re = are_ref[...]
    a_im = aim_ref[...]
    dt = jnp.exp(ldt_ref[...])
    for k in range(SCAN_LEVELS):
        step = float(2 ** k)
        mag = jnp.exp(step * dt * a_re)
        ang = step * dt * a_im
        pre_ref[k] = mag * jnp.cos(ang)
        pim_ref[k] = mag * jnp.sin(ang)
    abar_re = pre_ref[0]
    abar_im = pim_ref[0]
    den = a_re * a_re + a_im * a_im
    f_re = ((abar_re - 1.0) * a_re + abar_im * a_im) / den
    f_im = (abar_im * a_re - (abar_re - 1.0) * a_im) / den
    for c in range(S5_GROUP):
        bbre_ref[c] = f_re * bre_ref[c] - f_im * bim_ref[c]
        bbim_ref[c] = f_re * bim_ref[c] + f_im * bre_ref[c]


def _s5_steps(h, tile, wsu_ref, bre_ref, bim_ref, pre_ref, pim_ref, cre_ref, cim_ref,
              dskip_ref, wglu_ref, bglu_ref, buf_ref, st_ref, xs_ref):
    half_in = MIX_W // 2
    half_blocks = S5_LANE_BLOCKS // 2
    body = slice(SUBLANES, SUBLANES + tile)
    tail = slice(tile, tile + SUBLANES)
    head = slice(0, SUBLANES)
    yield MXU
    u = _dot(h, wsu_ref[...])
    ub = u.astype(BF16)
    for kt in range(2):
        yield MXU
        u_half = ub[:, kt * half_in:(kt + 1) * half_in]
        bu_re = _dot(u_half, bre_ref[kt])
        bu_im = _dot(u_half, bim_ref[kt])
        for j in range(half_blocks):
            lb = kt * half_blocks + j
            buf_ref[0, 0, lb, body, :] = bu_re[:, j * LANES:(j + 1) * LANES]
            buf_ref[0, 1, lb, body, :] = bu_im[:, j * LANES:(j + 1) * LANES]

    def scan_blocks(blocks):
        for lb in blocks:
            yield VPU
            p_re = pre_ref[lb]
            p_im = pim_ref[lb]
            for k, (src, dst) in enumerate(((0, 1), (1, 2), (2, 1))):
                shift = 2 ** k
                shifted = slice(SUBLANES - shift, SUBLANES - shift + tile)
                cur_re = buf_ref[src, 0, lb, body, :]
                cur_im = buf_ref[src, 1, lb, body, :]
                sh_re = buf_ref[src, 0, lb, shifted, :]
                sh_im = buf_ref[src, 1, lb, shifted, :]
                a_re = p_re[k:k + 1]
                a_im = p_im[k:k + 1]
                new_re = cur_re + a_re * sh_re - a_im * sh_im
                new_im = cur_im + a_re * sh_im + a_im * sh_re
                buf_ref[src, 0, lb, head, :] = buf_ref[src, 0, lb, tail, :]
                buf_ref[src, 1, lb, head, :] = buf_ref[src, 1, lb, tail, :]
                buf_ref[dst, 0, lb, body, :] = new_re
                buf_ref[dst, 1, lb, body, :] = new_im
        a8 = [(pre_ref[lb][3:4], pim_ref[lb][3:4]) for lb in blocks]
        state = [(st_ref[0, lb], st_ref[1, lb]) for lb in blocks]
        groups = tile // SUBLANES
        for g in range(groups):
            if g % (groups // 4) == 0:
                yield VPU
            grp = slice(SUBLANES * (g + 1), SUBLANES * (g + 2))
            for n, lb in enumerate(blocks):
                a_re, a_im = a8[n]
                s_re, s_im = state[n]
                x_re = buf_ref[1, 0, lb, grp, :] + a_re * s_re - a_im * s_im
                x_im = buf_ref[1, 1, lb, grp, :] + a_re * s_im + a_im * s_re
                buf_ref[1, 0, lb, grp, :] = x_re
                buf_ref[1, 1, lb, grp, :] = x_im
                state[n] = (x_re, x_im)
        yield VPU
        for n, lb in enumerate(blocks):
            st_ref[0, lb] = state[n][0]
            st_ref[1, lb] = state[n][1]
            xs_ref[0, lb] = buf_ref[1, 0, lb, body, :].astype(BF16)
            xs_ref[1, lb] = buf_ref[1, 1, lb, body, :].astype(BF16)

    ys = []
    for kt in range(2):
        blocks = list(range(kt * half_blocks, (kt + 1) * half_blocks))
        for n in range(0, half_blocks, SCAN_UNROLL):
            yield from scan_blocks(blocks[n:n + SCAN_UNROLL])
        yield MXU
        x_re = jnp.concatenate([xs_ref[0, lb] for lb in blocks], axis=1)
        x_im = jnp.concatenate([xs_ref[1, lb] for lb in blocks], axis=1)
        ys.append(_dot(x_re, cre_ref[kt]) - _dot(x_im, cim_ref[kt]))
    yield VPU
    y = jnp.concatenate(ys, axis=1) + dskip_ref[...] * u
    y = 0.5 * y * (1.0 + jnp.tanh(math.sqrt(2.0 / math.pi) * (y + 0.044715 * (y * y * y))))
    yield MXU
    gate = _dot(y.astype(BF16), wglu_ref[...]) + bglu_ref[...]
    yield VPU
    return (y * _sigmoid(gate)).astype(BF16)


def _gdn_steps(h, tile, wqkvz_ref, wda_ref, wdb_ref, cw_ref, alog_ref, dtb_ref, gnorm_ref, seg_ref,
               tril_ref, onesbd_ref, st_ref, cs_ref, od_ref, attn_ref, u_ref, wq_ref):
    qkv_w = 3 * MIX_W
    yield MXU
    da = _dot(h, wda_ref[...])
    db = _dot(h, wdb_ref[...])
    yield VPU
    log_alpha = -jnp.exp(alog_ref[...]) * _softplus(da + dtb_ref[...])
    beta = _sigmoid(db)
    row = lax.broadcasted_iota(jnp.int32, (tile, MIX_W), 0) % CHUNK
    key = lax.broadcasted_iota(jnp.int32, (tile, MIX_W), 1) % CHUNK
    yield MXU
    g = _dot01_lhs(tril_ref[...], log_alpha)
    yield MXU
    g_key = _dot01_lhs(onesbd_ref[...], jnp.where(row <= key, log_alpha, 0.0))
    yield MXU
    proj = _dot(h, wqkvz_ref[:, 0:qkv_w])
    yield VPU
    decay_incl = jnp.exp(jnp.where(row >= key, g - g_key, -jnp.inf))
    decay_strict = jnp.where(row > key, decay_incl, 0.0)
    exp_g = jnp.exp(g)
    conv = []
    for j in range(qkv_w // LANES):
        if j % 4 == 0:
            yield VPU
        lanes_j = slice(j * LANES, (j + 1) * LANES)
        cs_ref[j, SUBLANES:SUBLANES + tile, :] = proj[:, lanes_j]
        conv.append(_silu(_causal_conv(cs_ref, j, cw_ref, lanes_j, tile)))
    cs_ref[:, 0:SUBLANES, :] = cs_ref[:, tile:tile + SUBLANES, :]
    qkv = jnp.concatenate(conv, axis=1)
    q = qkv[:, 0:MIX_W]
    k_ = qkv[:, MIX_W:2 * MIX_W]
    v = qkv[:, 2 * MIX_W:]
    yield MXU
    seg = seg_ref[...]
    ss_q = _seg_mean(q * q, seg) * HEAD_DIM
    ss_k = _seg_mean(k_ * k_, seg) * HEAD_DIM
    yield VPU
    q = q * lax.rsqrt(ss_q + NORM_EPS) * HEAD_DIM ** -0.5
    k_ = k_ * lax.rsqrt(ss_k + NORM_EPS)
    rows = [slice(c * CHUNK, (c + 1) * CHUNK) for c in range(tile // CHUNK)]
    lanes = [slice(p * PAIR_W, (p + 1) * PAIR_W) for p in range(PAIRS)]
    g_last = [g[rs][CHUNK - 1:CHUNK] for rs in rows]
    k_dec = jnp.concatenate([k_[rs] * jnp.exp(gl - g[rs]) for rs, gl in zip(rows, g_last)], axis=0)
    k_b = k_.astype(BF16)
    q_b = q.astype(BF16)
    kd_b = k_dec.astype(BF16)
    qd_b = (q * exp_g).astype(BF16)
    vb_b = (v * beta).astype(BF16)
    kbg_b = (k_ * beta * exp_g).astype(BF16)
    eye_pair = (lax.broadcasted_iota(jnp.int32, (CHUNK, PAIR_W), 0)
                == lax.broadcasted_iota(jnp.int32, (CHUNK, PAIR_W), 1) % CHUNK).astype(F32)
    same_head = _same_head_mask()
    first_lane = lax.broadcasted_iota(jnp.int32, (CHUNK, PAIR_W), 1) < HEAD_DIM
    zero = jnp.zeros((CHUNK, PAIR_W), BF16)
    power, inv = {}, {}
    for c, rs in enumerate(rows):
        yield MXU
        for p, ls in enumerate(lanes):
            sc = _dot_nt(jnp.concatenate([k_b[rs, ls], q_b[rs, ls]], axis=0), _block_diag_pair(k_b[rs, ls]))
            power[c, p] = beta[rs, ls] * sc[0:CHUNK] * decay_strict[rs, ls]
            inv[c, p] = eye_pair - power[c, p]
            attn_ref[rs, ls] = (sc[CHUNK:] * decay_incl[rs, ls]).astype(BF16)
    for _ in range(int(math.log2(CHUNK)) - 1):
        for c in range(len(rows)):
            yield MXU
            for p in range(PAIRS):
                power[c, p] = _pair_matmul(power[c, p], power[c, p])
        for c in range(len(rows)):
            yield MXU
            for p in range(PAIRS):
                inv[c, p] = inv[c, p] + _pair_matmul(inv[c, p], power[c, p])
    for c, rs in enumerate(rows):
        yield MXU
        for p, ls in enumerate(lanes):
            vb = vb_b[rs, ls]
            kbg = kbg_b[rs, ls]
            rhs = jnp.concatenate(
                [jnp.concatenate([jnp.where(first_lane, vb, zero), jnp.where(first_lane, kbg, zero)], axis=1),
                 jnp.concatenate([jnp.where(first_lane, zero, vb), jnp.where(first_lane, zero, kbg)], axis=1)],
                axis=0)
            uw = _dot(inv[c, p].astype(BF16), rhs)
            u_ref[rs, ls] = uw[:, 0:PAIR_W]
            wq_ref[rs, ls] = uw[:, PAIR_W:].astype(BF16)
    for rs, gl in zip(rows, g_last):
        yield MXU
        state = [st_ref[p] for p in range(PAIRS)]
        ws_qs = [_dot(jnp.concatenate([wq_ref[rs, ls], qd_b[rs, ls]], axis=0), st.astype(BF16))
                 for ls, st in zip(lanes, state)]
        yield MXU
        v_new = [(u_ref[rs, ls] - wq[0:CHUNK]).astype(BF16) for ls, wq in zip(lanes, ws_qs)]
        upd = [_dot_tn(kd_b[rs, ls], vn) for ls, vn in zip(lanes, v_new)]
        for p, ls in enumerate(lanes):
            st_ref[p] = state[p] * jnp.exp(gl[:, ls]) + jnp.where(same_head, upd[p], 0.0)
        yield MXU
        for ls, wq, vn in zip(lanes, ws_qs, v_new):
            od_ref[rs, ls] = wq[CHUNK:] + _dot(attn_ref[rs, ls], _block_diag_pair(vn))
    yield MXU
    z = _dot(h, wqkvz_ref[:, qkv_w:])
    o_d = od_ref[...]
    ms = _seg_mean(o_d * o_d, seg)
    yield VPU
    return (o_d * lax.rsqrt(ms + NORM_EPS) * gnorm_ref[...] * _silu(z)).astype(BF16)


def _odd_kernel(x_ref, gain_ref,
                wsu_ref, bre_ref, bim_ref, pre_ref, pim_ref, cre_ref, cim_ref, dskip_ref, wglu_ref, bglu_ref,
                wqkvz_ref, wda_ref, wdb_ref, cw_ref, alog_ref, dtb_ref, gnorm_ref, seg_ref, tril_ref,
                onesbd_ref, woc_ref, wod_ref,
                o_ref,
                buf_ref, sst_ref, xs_ref, gst_ref, cs_ref, od_ref, attn_ref, u_ref, wq_ref):
    tile = x_ref.shape[1]

    @pl.when(pl.program_id(1) == 0)
    def _():
        sst_ref[...] = jnp.zeros_like(sst_ref)
        buf_ref[:, :, :, 0:SUBLANES, :] = jnp.zeros((3, 2, S5_LANE_BLOCKS, SUBLANES, LANES), F32)
        gst_ref[...] = jnp.zeros_like(gst_ref)
        cs_ref[:, 0:SUBLANES, :] = jnp.zeros((3 * MIX_W // LANES, SUBLANES, LANES), F32)

    x = x_ref[0]
    h = _rms_rows(x, gain_ref[...]).astype(BF16)
    o_c, o_d = _interleave(
        _s5_steps(h, tile, wsu_ref, bre_ref, bim_ref, pre_ref, pim_ref, cre_ref, cim_ref,
                  dskip_ref, wglu_ref, bglu_ref, buf_ref, sst_ref, xs_ref),
        _gdn_steps(h, tile, wqkvz_ref, wda_ref, wdb_ref, cw_ref, alog_ref, dtb_ref, gnorm_ref, seg_ref,
                   tril_ref, onesbd_ref, gst_ref, cs_ref, od_ref, attn_ref, u_ref, wq_ref))
    o_ref[0] = x + _dot(o_c, woc_ref[...]) + _dot(o_d, wod_ref[...])


def _odd_layer(x, gain, w_in, a_re, a_im, log_dt, b_re, b_im, c_re, c_im, d_skip, w_glu, b_glu,
               conv_w, a_log, dt_bias, gdn_norm, w_out):
    bsz, seq, _ = x.shape
    tile = min(MIX_TILE, seq)
    su1 = MIX_W
    z1 = su1 + 4 * MIX_W
    gs = (S5_GROUPS, S5_STATE)
    pow_re, pow_im, bb_re, bb_im = pl.pallas_call(
        _s5_prep_kernel,
        out_shape=[jax.ShapeDtypeStruct((SCAN_LEVELS,) + gs, F32)] * 2
        + [jax.ShapeDtypeStruct((S5_GROUP,) + gs, F32)] * 2,
        name="s5_discretize",
    )(a_re, a_im, log_dt.reshape(S5_GROUPS, 1), b_re.transpose(2, 0, 1), b_im.transpose(2, 0, 1))
    half_g = S5_GROUPS // 2
    eye = jnp.eye(half_g, dtype=F32)

    def in_blocks(bb):
        t = bb.reshape(S5_GROUP, 2, half_g, S5_STATE)
        return jnp.einsum('ckgp,gh->kgchp', t, eye).reshape(2, half_g * S5_GROUP, half_g * S5_STATE).astype(BF16)

    def out_blocks(cc):
        t = cc.reshape(2, half_g, S5_GROUP, S5_STATE)
        return jnp.einsum('kgop,gh->kgpho', t, eye).reshape(2, half_g * S5_STATE, half_g * S5_GROUP).astype(BF16)

    def lane_blocks(pw):
        return pw.reshape(SCAN_LEVELS, S5_LANE_BLOCKS, LANES).transpose(1, 0, 2)

    rep = lambda w: jnp.repeat(w, HEAD_DIM, axis=-1)
    args = (x, gain.reshape(1, D_MODEL),
            w_in[:, :su1].astype(BF16), in_blocks(bb_re), in_blocks(bb_im),
            lane_blocks(pow_re), lane_blocks(pow_im), out_blocks(c_re), out_blocks(c_im),
            d_skip.reshape(1, MIX_W), w_glu.astype(BF16), b_glu.reshape(1, MIX_W),
            w_in[:, su1:z1].astype(BF16), rep(w_in[:, z1:z1 + HEADS]).astype(BF16),
            rep(w_in[:, z1 + HEADS:]).astype(BF16), conv_w, rep(a_log).reshape(1, MIX_W),
            rep(dt_bias).reshape(1, MIX_W), _head_tile(gdn_norm, HEADS), _seg_matrix(),
            _block_tril(tile), _block_ones(tile), w_out[:MIX_W].astype(BF16), w_out[MIX_W:].astype(BF16))
    in_specs = [_tile_spec(tile, D_MODEL)] + [_const_spec(a.shape) for a in args[1:]]
    return pl.pallas_call(
        _odd_kernel,
        grid=(bsz, seq // tile),
        in_specs=in_specs,
        out_specs=_tile_spec(tile, D_MODEL),
        out_shape=jax.ShapeDtypeStruct(x.shape, F32),
        scratch_shapes=[pltpu.VMEM((3, 2, S5_LANE_BLOCKS, SUBLANES + tile, LANES), F32),
                        pltpu.VMEM((2, S5_LANE_BLOCKS, SUBLANES, LANES), F32),
                        pltpu.VMEM((2, S5_LANE_BLOCKS, tile, LANES), BF16),
                        pltpu.VMEM((PAIRS, PAIR_W, PAIR_W), F32),
                        pltpu.VMEM((3 * MIX_W // LANES, SUBLANES + tile, LANES), F32),
                        pltpu.VMEM((tile, MIX_W), F32),
                        pltpu.VMEM((tile, MIX_W), BF16),
                        pltpu.VMEM((tile, MIX_W), F32),
                        pltpu.VMEM((tile, MIX_W), BF16)],
        compiler_params=_params(),
        name="s5_gdn_mixer",
    )(*args)


def kernel(x, norm_mix, norm_ffn, w_in_even, w_gla_gate, b_gla_gate, gla_out_norm, swa_q_norm, swa_k_norm, swa_sinks, w_out_even, w_in_odd, s5_a_re, s5_a_im, s5_log_dt, s5_b_re, s5_b_im, s5_c_re, s5_c_im, s5_d, s5_w_glu, s5_b_glu, gdn_conv_w, gdn_a_log, gdn_dt_bias, gdn_out_norm, w_out_odd, w_ffn_up, ffn_conv_w, ffn_conv_b, w_ffn_down):
    depth = norm_mix.shape[0]
    for layer in range(depth):
        i = layer // 2
        if layer % 2 == 0:
            x = _even_layer(x, norm_mix[layer], w_in_even[i], w_gla_gate[i], b_gla_gate[i],
                            gla_out_norm[i], swa_q_norm[i], swa_k_norm[i], swa_sinks[i], w_out_even[i])
        else:
            x = _odd_layer(x, norm_mix[layer], w_in_odd[i], s5_a_re[i], s5_a_im[i], s5_log_dt[i],
                           s5_b_re[i], s5_b_im[i], s5_c_re[i], s5_c_im[i], s5_d[i], s5_w_glu[i],
                           s5_b_glu[i], gdn_conv_w[i], gdn_a_log[i], gdn_dt_bias[i], gdn_out_norm[i],
                           w_out_odd[i])
        x = _ffn_layer(x, norm_ffn[layer], w_ffn_up[layer], ffn_conv_w[layer], ffn_conv_b[layer],
                       w_ffn_down[layer])
    return x
```

```python
import math

import jax
import jax.numpy as jnp
from jax import lax
from jax.experimental import pallas as pl
from jax.experimental.pallas import tpu as pltpu

F32 = jnp.float32
BF16 = jnp.bfloat16

D_MODEL = 1024
HEAD_DIM = 64
HEADS = 8
MIX_W = HEADS * HEAD_DIM
PAIRS = HEADS // 2
PAIR_W = 2 * HEAD_DIM
GLA_RANK = 16
GLA_GATE_NORM = 16.0
CHUNK = 64
SWA_BLOCK = 128
SWA_KV_HEADS = 2
S5_GROUP = 16
S5_GROUPS = 32
S5_STATE = 64
S5_LANES = S5_GROUPS * S5_STATE
GDN_CONV = 4
FFN_DIM = 2816
FFN_CONV = 3
NORM_EPS = 1e-6
LANES = 128
SUBLANES = 8
MXU_COLS = 256
VMEM_LIMIT = 56 * 1024 * 1024
S5_LANE_BLOCKS = S5_LANES // LANES

MIX_TILE = 256
FFN_TILE = 512
FFN_COLS = MXU_COLS
SCAN_LEVELS = int(math.log2(SUBLANES)) + 1
SCAN_UNROLL = 8
MXU = "mxu"
VPU = "vpu"
WHOLE = 1 << 20
EVEN_WEIGHTS = (WHOLE, WHOLE)
ODD_WEIGHTS = (1, 2)
GDN_GROUP = 4


def _dot(a, b):
    return jnp.dot(a, b, preferred_element_type=F32)


def _dot_nt(a, b):
    return lax.dot_general(a, b, (((1,), (1,)), ((), ())), preferred_element_type=F32)


def _dot_tn(a, b):
    return lax.dot_general(a, b, (((0,), (0,)), ((), ())), preferred_element_type=F32)


def _split3(x):
    hi = x.astype(BF16)
    r1 = x - hi.astype(F32)
    mid = r1.astype(BF16)
    lo = (r1 - mid.astype(F32)).astype(BF16)
    return hi, mid, lo


def _dot01_lhs(m01, x):
    hi, mid, lo = _split3(x)
    return _dot(m01, hi) + _dot(m01, mid) + _dot(m01, lo)


def _sigmoid(x):
    return 1.0 / (1.0 + jnp.exp(-x))


def _silu(x):
    return x * _sigmoid(x)


def _softplus(x):
    return jnp.maximum(x, 0.0) + jnp.log1p(jnp.exp(-jnp.abs(x)))


def _rms_rows(x, gain_row):
    ms = jnp.mean(x * x, axis=-1, keepdims=True)
    return x * lax.rsqrt(ms + NORM_EPS) * gain_row


def _seg_mean(y, seg01):
    return _dot(y.astype(BF16), seg01) * (1.0 / HEAD_DIM)


def _block_diag_pair(v):
    first = lax.broadcasted_iota(jnp.int32, v.shape, 1) < HEAD_DIM
    zero = jnp.zeros_like(v)
    return jnp.concatenate([jnp.where(first, v, zero), jnp.where(first, zero, v)], axis=0)


def _pair_matmul(a, b):
    return _dot(a.astype(BF16), _block_diag_pair(b.astype(BF16)))


def _same_head_mask():
    r = lax.broadcasted_iota(jnp.int32, (PAIR_W, PAIR_W), 0)
    c = lax.broadcasted_iota(jnp.int32, (PAIR_W, PAIR_W), 1)
    return (r < HEAD_DIM) == (c < HEAD_DIM)


def _causal_conv(u_ref, j, taps_ref, lanes, tile, acc=None):
    taps = taps_ref.shape[0]
    for k in range(taps):
        off = SUBLANES - (taps - 1) + k
        term = taps_ref[k:k + 1, lanes] * u_ref[j, off:off + tile, :]
        acc = term if acc is None else acc + term
    return acc


def _interleave(streams, weights):
    results = [None] * len(streams)
    live = list(range(len(streams)))
    while live:
        for i in list(live):
            for _ in range(weights[i]):
                try:
                    next(streams[i])
                except StopIteration as stop:
                    results[i] = stop.value
                    live.remove(i)
                    break
    return results


def _const_spec(shape):
    nd = len(shape)
    return pl.BlockSpec(shape, lambda b, l: (0,) * nd, pipeline_mode=pl.Buffered(1))


def _tile_spec(tile, width):
    return pl.BlockSpec((1, tile, width), lambda b, l: (b, l, 0))


def _params():
    return pltpu.CompilerParams(dimension_semantics=("arbitrary", "arbitrary"),
                                vmem_limit_bytes=VMEM_LIMIT)


def _ffn_kernel(x_ref, gain_ref, wa_ref, wg_ref, cwa_ref, cwg_ref, cba_ref, cbg_ref, wd_ref,
                o_ref, ua_ref, ug_ref, act_ref):
    tile = x_ref.shape[1]
    blocks = FFN_COLS // LANES

    @pl.when(pl.program_id(1) == 0)
    def _():
        ua_ref[:, 0:SUBLANES, :] = jnp.zeros((FFN_DIM // LANES, SUBLANES, LANES), F32)
        ug_ref[:, 0:SUBLANES, :] = jnp.zeros((FFN_DIM // LANES, SUBLANES, LANES), F32)

    x = x_ref[0]
    h = _rms_rows(x, gain_ref[...]).astype(BF16)
    for c in range(FFN_DIM // FFN_COLS):
        sl = slice(c * FFN_COLS, (c + 1) * FFN_COLS)
        ua = _dot(h, wa_ref[:, sl])
        ug = _dot(h, wg_ref[:, sl])
        for t in range(blocks):
            j = c * blocks + t
            lanes = slice(j * LANES, (j + 1) * LANES)
            ua_ref[j, SUBLANES:SUBLANES + tile, :] = ua[:, t * LANES:(t + 1) * LANES]
            ug_ref[j, SUBLANES:SUBLANES + tile, :] = ug[:, t * LANES:(t + 1) * LANES]
            a = _causal_conv(ua_ref, j, cwa_ref, lanes, tile, cba_ref[:, lanes])
            g = _causal_conv(ug_ref, j, cwg_ref, lanes, tile, cbg_ref[:, lanes])
            act_ref[:, lanes] = (_silu(g) * a).astype(BF16)
    ua_ref[:, 0:SUBLANES, :] = ua_ref[:, tile:tile + SUBLANES, :]
    ug_ref[:, 0:SUBLANES, :] = ug_ref[:, tile:tile + SUBLANES, :]
    o_ref[0] = x + _dot(act_ref[...], wd_ref[...])


def _layer_spec(rows, cols, layer, col_block=0):
    return pl.BlockSpec((None, rows, cols), lambda b, l: (layer, 0, col_block),
                        pipeline_mode=pl.Buffered(1))


def _ffn_layer(x, layer, gains, w_up, conv_w, conv_b, w_down):
    bsz, seq, _ = x.shape
    tile = min(FFN_TILE, seq)
    args = (x, gains, w_up, w_up, conv_w, conv_w, conv_b, conv_b, w_down)
    in_specs = [_tile_spec(tile, D_MODEL),
                _layer_spec(1, D_MODEL, layer),
                _layer_spec(D_MODEL, FFN_DIM, layer, 0), _layer_spec(D_MODEL, FFN_DIM, layer, 1),
                _layer_spec(FFN_CONV, FFN_DIM, layer, 0), _layer_spec(FFN_CONV, FFN_DIM, layer, 1),
                _layer_spec(1, FFN_DIM, layer, 0), _layer_spec(1, FFN_DIM, layer, 1),
                _layer_spec(FFN_DIM, D_MODEL, layer)]
    return pl.pallas_call(
        _ffn_kernel,
        grid=(bsz, seq // tile),
        in_specs=in_specs,
        out_specs=_tile_spec(tile, D_MODEL),
        out_shape=jax.ShapeDtypeStruct(x.shape, F32),
        scratch_shapes=[pltpu.VMEM((FFN_DIM // LANES, SUBLANES + tile, LANES), F32),
                        pltpu.VMEM((FFN_DIM // LANES, SUBLANES + tile, LANES), F32),
                        pltpu.VMEM((tile, FFN_DIM), BF16)],
        compiler_params=_params(),
        name="conv_ffn",
    )(*args)


def _gla_steps(h, tile, wmain_ref, wglr_ref, wgate_ref, bgate_ref, glan_ref, seg_ref, tril_ref,
               st_ref, oa_ref):
    yield MXU
    proj = _dot(h, wmain_ref[:, 0:4 * MIX_W])
    gq = proj[:, 0 * MIX_W:1 * MIX_W]
    gk = proj[:, 1 * MIX_W:2 * MIX_W]
    gv = proj[:, 2 * MIX_W:3 * MIX_W]
    gr = proj[:, 3 * MIX_W:4 * MIX_W]
    glr = _dot(h, wglr_ref[...])
    gate = _dot(glr.astype(BF16), wgate_ref[...]) + bgate_ref[...]
    yield VPU
    log_g = (jnp.minimum(gate, 0.0) - jnp.log1p(jnp.exp(-jnp.abs(gate)))) * (1.0 / GLA_GATE_NORM)
    yield MXU
    cum = _dot01_lhs(tril_ref[...], log_g)
    yield VPU
    q_pos = (gq * HEAD_DIM ** -0.5) * jnp.exp(cum)
    k_neg = gk * jnp.exp(-cum)
    row = lax.broadcasted_iota(jnp.int32, (CHUNK, PAIR_W), 0)
    lane = lax.broadcasted_iota(jnp.int32, (CHUNK, PAIR_W), 1)
    causal = row >= (lane % CHUNK)
    same_head = _same_head_mask()
    rows = [slice(c * CHUNK, (c + 1) * CHUNK) for c in range(tile // CHUNK)]
    lanes = [slice(p * PAIR_W, (p + 1) * PAIR_W) for p in range(PAIRS)]
    cum_last = [cum[rs][CHUNK - 1:CHUNK] for rs in rows]
    k_dec = jnp.concatenate([gk[rs] * jnp.exp(cl - cum[rs]) for rs, cl in zip(rows, cum_last)], axis=0)
    q_b = q_pos.astype(BF16)
    kn_b = k_neg.astype(BF16)
    kd_b = k_dec.astype(BF16)
    v_b = gv.astype(BF16)
    scores, o_intra, upd = {}, {}, {}
    for c, rs in enumerate(rows):
        yield MXU
        for p, ls in enumerate(lanes):
            s = _dot_nt(q_b[rs, ls], _block_diag_pair(kn_b[rs, ls]))
            scores[c, p] = jnp.where(causal, s, 0.0).astype(BF16)
    for c, rs in enumerate(rows):
        yield MXU
        for p, ls in enumerate(lanes):
            upd[c, p] = jnp.where(same_head, _dot_tn(v_b[rs, ls], kd_b[rs, ls]), 0.0)
    for c, rs in enumerate(rows):
        yield MXU
        for p, ls in enumerate(lanes):
            o_intra[c, p] = _dot(scores[c, p], _block_diag_pair(v_b[rs, ls]))
    yield VPU
    states = {}
    for p, ls in enumerate(lanes):
        state_t = st_ref[p]
        for c, cl in enumerate(cum_last):
            states[c, p] = state_t
            state_t = state_t * jnp.exp(cl[:, ls]) + upd[c, p]
        st_ref[p] = state_t
    for c, rs in enumerate(rows):
        yield MXU
        for p, ls in enumerate(lanes):
            oa_ref[rs, ls] = o_intra[c, p] + _dot_nt(q_b[rs, ls], states[c, p].astype(BF16))
    yield MXU
    o_a = oa_ref[...]
    ms = _seg_mean(o_a * o_a, seg_ref[...])
    yield VPU
    return (o_a * lax.rsqrt(ms + NORM_EPS) * glan_ref[...] * _silu(gr)).astype(BF16)


def _swa_steps(h, tile, seq_tile, sinks_ref, wmain_ref, wkv_ref, qn_ref, kn_ref, seg_ref, bias_ref,
               kvs_ref, ob_ref):
    yield MXU
    sq = _dot(h, wmain_ref[:, 4 * MIX_W:5 * MIX_W])
    kv = _dot(h, wkv_ref[...])
    seg = seg_ref[...]
    yield MXU
    ms_q = _seg_mean(sq * sq, seg)
    sk = kv[:, 0:PAIR_W]
    ms_k = _seg_mean(sk * sk, seg[0:PAIR_W, 0:PAIR_W])
    yield VPU
    qn = (sq * lax.rsqrt(ms_q + NORM_EPS) * qn_ref[...] * HEAD_DIM ** -0.5).astype(BF16)
    kvs_ref[SWA_BLOCK:SWA_BLOCK + tile, 0:PAIR_W] = sk * lax.rsqrt(ms_k + NORM_EPS) * kn_ref[...]
    kvs_ref[SWA_BLOCK:SWA_BLOCK + tile, PAIR_W:2 * PAIR_W] = kv[:, PAIR_W:2 * PAIR_W]
    skey = lax.broadcasted_iota(jnp.int32, (2 * SWA_BLOCK, 2 * SWA_BLOCK), 1)
    top_col = lax.broadcasted_iota(jnp.int32, (2 * SWA_BLOCK, 1), 0) < SWA_BLOCK
    ones_cols = jnp.ones((2 * SWA_BLOCK, PAIR_W), BF16)
    first_lane = lax.broadcasted_iota(jnp.int32, (2 * SWA_BLOCK, PAIR_W), 1) < HEAD_DIM
    no_prev = jnp.where(skey < SWA_BLOCK, jnp.where(seq_tile > 0, 0.0, -jnp.inf), 0.0)
    for j in range(tile // SWA_BLOCK):
        ks = slice(j * SWA_BLOCK, (j + 2) * SWA_BLOCK)
        keys = kvs_ref[ks, 0:PAIR_W].astype(BF16)
        vals = kvs_ref[ks, PAIR_W:2 * PAIR_W].astype(BF16)
        zero_v = jnp.zeros_like(vals)
        vals0 = jnp.concatenate([jnp.where(first_lane, vals, zero_v), ones_cols], axis=1)
        vals1 = jnp.concatenate([jnp.where(first_lane, zero_v, vals), ones_cols], axis=1)
        for p in range(PAIRS):
            yield MXU
            qs = _block_diag_pair(qn[j * SWA_BLOCK:(j + 1) * SWA_BLOCK, p * PAIR_W:(p + 1) * PAIR_W])
            s = _dot_nt(qs, keys)
            yield VPU
            s = s + bias_ref[p]
            if j == 0:
                s = s + no_prev
            sink = jnp.where(top_col, sinks_ref[p], sinks_ref[p + PAIRS])
            m = jnp.maximum(jnp.max(s, axis=-1, keepdims=True), sink)
            pe = jnp.exp(s - m).astype(BF16)
            sink_e = jnp.exp(sink - m)
            yield MXU
            r0 = _dot(pe[0:SWA_BLOCK], vals0)
            r1 = _dot(pe[SWA_BLOCK:], vals1)
            o_pair = (r0[:, 0:PAIR_W] / (r0[:, PAIR_W:] + sink_e[0:SWA_BLOCK])
                      + r1[:, 0:PAIR_W] / (r1[:, PAIR_W:] + sink_e[SWA_BLOCK:]))
            ob_ref[j * SWA_BLOCK:(j + 1) * SWA_BLOCK, p * PAIR_W:(p + 1) * PAIR_W] = o_pair.astype(BF16)
    kvs_ref[0:SWA_BLOCK, :] = kvs_ref[tile:tile + SWA_BLOCK, :]
    return ob_ref[...]


def _even_kernel(sinks_ref, x_ref, gain_ref, wmain_ref, wkv_ref, wglr_ref, wgate_ref, bgate_ref,
                 glan_ref, qn_ref, kn_ref, seg_ref, tril_ref, bias_ref, woa_ref, wob_ref,
                 o_ref, st_ref, kvs_ref, oa_ref, ob_ref):
    tile = x_ref.shape[1]
    seq_tile = pl.program_id(1)

    @pl.when(seq_tile == 0)
    def _():
        st_ref[...] = jnp.zeros_like(st_ref)
        kvs_ref[0:SWA_BLOCK, :] = jnp.zeros((SWA_BLOCK, 2 * PAIR_W), F32)

    x = x_ref[0]
    h = _rms_rows(x, gain_ref[...]).astype(BF16)
    o_a, o_b = _interleave(
        [_gla_steps(h, tile, wmain_ref, wglr_ref, wgate_ref, bgate_ref, glan_ref, seg_ref, tril_ref,
                    st_ref, oa_ref),
         _swa_steps(h, tile, seq_tile, sinks_ref, wmain_ref, wkv_ref, qn_ref, kn_ref, seg_ref, bias_ref,
                    kvs_ref, ob_ref)],
        EVEN_WEIGHTS)
    o_ref[0] = x + _dot(o_a, woa_ref[...]) + _dot(o_b, wob_ref[...])


def _block_tril(tile):
    r = jnp.arange(tile)
    same = (r[:, None] // CHUNK) == (r[None, :] // CHUNK)
    return (same & (r[:, None] >= r[None, :])).astype(BF16)


def _block_ones(tile):
    r = jnp.arange(tile)
    return ((r[:, None] // CHUNK) == (r[None, :] // CHUNK)).astype(BF16)


def _seg_matrix():
    r = jnp.arange(MIX_W)
    return ((r[:, None] // HEAD_DIM) == (r[None, :] // HEAD_DIM)).astype(BF16)


def _swa_bias():
    r = jnp.arange(2 * SWA_BLOCK)
    dist = (r[:, None] % SWA_BLOCK) + SWA_BLOCK - r[None, :]
    valid = (dist >= 0) & (dist < SWA_BLOCK)
    head = jnp.arange(PAIRS)[:, None, None] + jnp.where(r < SWA_BLOCK, 0, PAIRS)[None, :, None]
    slope = jnp.exp2(-(head + 1).astype(F32))
    return jnp.where(valid[None], -slope * dist[None].astype(F32), -jnp.inf)


def _head_tile(v, reps):
    return jnp.tile(v.astype(F32), reps).reshape(1, reps * v.shape[0])


def _even_layer(x, gain, w_in, w_gate, b_gate, gla_norm, q_norm, k_norm, sinks, w_out):
    bsz, seq, _ = x.shape
    tile = min(2 * MIX_TILE, seq)
    order = jnp.array([h for p in range(PAIRS) for h in (p, p + PAIRS)])
    c0 = 4 * MIX_W
    glr0 = c0
    sq0 = c0 + GLA_RANK
    sk0 = sq0 + MIX_W
    w_sq = w_in[:, sq0:sk0].reshape(D_MODEL, HEADS, HEAD_DIM)[:, order].reshape(D_MODEL, MIX_W)
    w_main = jnp.concatenate([w_in[:, :c0], w_sq], axis=1).astype(BF16)
    w_kv = w_in[:, sk0:].astype(BF16)
    w_glr = jnp.pad(w_in[:, glr0:sq0], ((0, 0), (0, LANES - GLA_RANK))).astype(BF16)
    w_gate_p = jnp.pad(w_gate, ((0, LANES - GLA_RANK), (0, 0))).astype(BF16)
    w_out_a = w_out[:MIX_W].astype(BF16)
    w_out_b = w_out[MIX_W:].reshape(HEADS, HEAD_DIM, D_MODEL)[order].reshape(MIX_W, D_MODEL).astype(BF16)
    args = (x, gain.reshape(1, D_MODEL), w_main, w_kv, w_glr, w_gate_p, b_gate.reshape(1, MIX_W),
            _head_tile(gla_norm, HEADS), _head_tile(q_norm, HEADS), _head_tile(k_norm, SWA_KV_HEADS),
            _seg_matrix(), _block_tril(tile), _swa_bias(), w_out_a, w_out_b)
    in_specs = ([pl.BlockSpec(memory_space=pltpu.SMEM), _tile_spec(tile, D_MODEL)]
                + [_const_spec(a.shape) for a in args[1:]])
    return pl.pallas_call(
        _even_kernel,
        grid=(bsz, seq // tile),
        in_specs=in_specs,
        out_specs=_tile_spec(tile, D_MODEL),
        out_shape=jax.ShapeDtypeStruct(x.shape, F32),
        scratch_shapes=[pltpu.VMEM((PAIRS, PAIR_W, PAIR_W), F32),
                        pltpu.VMEM((SWA_BLOCK + tile, 2 * PAIR_W), F32),
                        pltpu.VMEM((tile, MIX_W), F32),
                        pltpu.VMEM((tile, MIX_W), BF16)],
        compiler_params=_params(),
        name="gla_swa_mixer",
    )(sinks.astype(F32), *args)


def _s5_prep_kernel(are_ref, aim_ref, ldt_ref, bre_ref, bim_ref, pre_ref, pim_ref, bbre_ref, bbim_ref):
    a_re = are_ref[...]
    a_im = aim_ref[...]
    dt = jnp.exp(ldt_ref[...])
    for k in range(SCAN_LEVELS):
        step = float(2 ** k)
        mag = jnp.exp(step * dt * a_re)
        ang = step * dt * a_im
        pre_ref[k] = mag * jnp.cos(ang)
        pim_ref[k] = mag * jnp.sin(ang)
    abar_re = pre_ref[0]
    abar_im = pim_ref[0]
    den = a_re * a_re + a_im * a_im
    f_re = ((abar_re - 1.0) * a_re + abar_im * a_im) / den
    f_im = (abar_im * a_re - (abar_re - 1.0) * a_im) / den
    for c in range(S5_GROUP):
        bbre_ref[c] = f_re * bre_ref[c] - f_im * bim_ref[c]
        bbim_ref[c] = f_re * bim_ref[c] + f_im * bre_ref[c]


def _s5_steps(h, tile, wsu_ref, bre_ref, bim_ref, pre_ref, pim_ref, cre_ref, cim_ref,
              dskip_ref, wglu_ref, bglu_ref, buf_ref, st_ref, xs_ref):
    half_in = MIX_W // 2
    half_blocks = S5_LANE_BLOCKS // 2
    body = slice(SUBLANES, SUBLANES + tile)
    tail = slice(tile, tile + SUBLANES)
    head = slice(0, SUBLANES)
    yield MXU
    u = _dot(h, wsu_ref[...])
    ub = u.astype(BF16)
    for kt in range(2):
        yield MXU
        u_half = ub[:, kt * half_in:(kt + 1) * half_in]
        bu_re = _dot(u_half, bre_ref[kt])
        bu_im = _dot(u_half, bim_ref[kt])
        for j in range(half_blocks):
            lb = kt * half_blocks + j
            buf_ref[0, 0, lb, body, :] = bu_re[:, j * LANES:(j + 1) * LANES]
            buf_ref[0, 1, lb, body, :] = bu_im[:, j * LANES:(j + 1) * LANES]

    def scan_blocks(blocks):
        for lb in blocks:
            yield VPU
            p_re = pre_ref[lb]
            p_im = pim_ref[lb]
            for k, (src, dst) in enumerate(((0, 1), (1, 2), (2, 1))):
                shift = 2 ** k
                shifted = slice(SUBLANES - shift, SUBLANES - shift + tile)
                cur_re = buf_ref[src, 0, lb, body, :]
                cur_im = buf_ref[src, 1, lb, body, :]
                sh_re = buf_ref[src, 0, lb, shifted, :]
                sh_im = buf_ref[src, 1, lb, shifted, :]
                a_re = p_re[k:k + 1]
                a_im = p_im[k:k + 1]
                new_re = cur_re + a_re * sh_re - a_im * sh_im
                new_im = cur_im + a_re * sh_im + a_im * sh_re
                buf_ref[src, 0, lb, head, :] = buf_ref[src, 0, lb, tail, :]
                buf_ref[src, 1, lb, head, :] = buf_ref[src, 1, lb, tail, :]
                buf_ref[dst, 0, lb, body, :] = new_re
                buf_ref[dst, 1, lb, body, :] = new_im
        a8 = [(pre_ref[lb][3:4], pim_ref[lb][3:4]) for lb in blocks]
        state = [(st_ref[0, lb], st_ref[1, lb]) for lb in blocks]
        groups = tile // SUBLANES
        for g in range(groups):
            if g % (groups // 4) == 0:
                yield VPU
            grp = slice(SUBLANES * (g + 1), SUBLANES * (g + 2))
            for n, lb in enumerate(blocks):
                a_re, a_im = a8[n]
                s_re, s_im = state[n]
                x_re = buf_ref[1, 0, lb, grp, :] + a_re * s_re - a_im * s_im
                x_im = buf_ref[1, 1, lb, grp, :] + a_re * s_im + a_im * s_re
                buf_ref[1, 0, lb, grp, :] = x_re
                buf_ref[1, 1, lb, grp, :] = x_im
                state[n] = (x_re, x_im)
        yield VPU
        for n, lb in enumerate(blocks):
            st_ref[0, lb] = state[n][0]
            st_ref[1, lb] = state[n][1]
            xs_ref[0, lb] = buf_ref[1, 0, lb, body, :].astype(BF16)
            xs_ref[1, lb] = buf_ref[1, 1, lb, body, :].astype(BF16)

    ys = []
    for kt in range(2):
        blocks = list(range(kt * half_blocks, (kt + 1) * half_blocks))
        for n in range(0, half_blocks, SCAN_UNROLL):
            yield from scan_blocks(blocks[n:n + SCAN_UNROLL])
        yield MXU
        x_re = jnp.concatenate([xs_ref[0, lb] for lb in blocks], axis=1)
        x_im = jnp.concatenate([xs_ref[1, lb] for lb in blocks], axis=1)
        ys.append(_dot(x_re, cre_ref[kt]) - _dot(x_im, cim_ref[kt]))
    yield VPU
    y = jnp.concatenate(ys, axis=1) + dskip_ref[...] * u
    y = 0.5 * y * (1.0 + jnp.tanh(math.sqrt(2.0 / math.pi) * (y + 0.044715 * (y * y * y))))
    yield MXU
    gate = _dot(y.astype(BF16), wglu_ref[...]) + bglu_ref[...]
    yield VPU
    return (y * _sigmoid(gate)).astype(BF16)


def _gdn_steps(h, tile, wqkvz_ref, wda_ref, wdb_ref, cw_ref, alog_ref, dtb_ref, gnorm_ref, seg_ref,
               tril_ref, onesbd_ref, st_ref, cs_ref, od_ref, attn_ref, uw_ref, ku_ref, kwq_ref):
    qkv_w = 3 * MIX_W
    yield MXU
    da = _dot(h, wda_ref[...])
    db = _dot(h, wdb_ref[...])
    yield VPU
    log_alpha = -jnp.exp(alog_ref[...]) * _softplus(da + dtb_ref[...])
    beta = _sigmoid(db)
    row = lax.broadcasted_iota(jnp.int32, (tile, MIX_W), 0) % CHUNK
    key = lax.broadcasted_iota(jnp.int32, (tile, MIX_W), 1) % CHUNK
    yield MXU
    g = _dot01_lhs(tril_ref[...], log_alpha)
    yield MXU
    g_key = _dot01_lhs(onesbd_ref[...], jnp.where(row <= key, log_alpha, 0.0))
    yield MXU
    proj = _dot(h, wqkvz_ref[:, 0:qkv_w])
    yield VPU
    decay_incl = jnp.exp(jnp.where(row >= key, g - g_key, -jnp.inf))
    decay_strict = jnp.where(row > key, decay_incl, 0.0)
    exp_g = jnp.exp(g)
    conv = []
    for j in range(qkv_w // LANES):
        if j % 4 == 0:
            yield VPU
        lanes_j = slice(j * LANES, (j + 1) * LANES)
        cs_ref[j, SUBLANES:SUBLANES + tile, :] = proj[:, lanes_j]
        conv.append(_silu(_causal_conv(cs_ref, j, cw_ref, lanes_j, tile)))
    cs_ref[:, 0:SUBLANES, :] = cs_ref[:, tile:tile + SUBLANES, :]
    qkv = jnp.concatenate(conv, axis=1)
    q = qkv[:, 0:MIX_W]
    k_ = qkv[:, MIX_W:2 * MIX_W]
    v = qkv[:, 2 * MIX_W:]
    yield MXU
    seg = seg_ref[...]
    ss_q = _seg_mean(q * q, seg) * HEAD_DIM
    ss_k = _seg_mean(k_ * k_, seg) * HEAD_DIM
    yield VPU
    q = q * lax.rsqrt(ss_q + NORM_EPS) * HEAD_DIM ** -0.5
    k_ = k_ * lax.rsqrt(ss_k + NORM_EPS)
    rows = [slice(c * CHUNK, (c + 1) * CHUNK) for c in range(tile // CHUNK)]
    lanes = [slice(p * PAIR_W, (p + 1) * PAIR_W) for p in range(PAIRS)]
    g_last = [g[rs][CHUNK - 1:CHUNK] for rs in rows]
    k_dec = jnp.concatenate([k_[rs] * jnp.exp(gl - g[rs]) for rs, gl in zip(rows, g_last)], axis=0)
    k_b = k_.astype(BF16)
    q_b = q.astype(BF16)
    kd_b = k_dec.astype(BF16)
    qd_b = (q * exp_g).astype(BF16)
    vb_b = (v * beta).astype(BF16)
    kbg_b = (k_ * beta * exp_g).astype(BF16)
    eye_pair = (lax.broadcasted_iota(jnp.int32, (CHUNK, PAIR_W), 0)
                == lax.broadcasted_iota(jnp.int32, (CHUNK, PAIR_W), 1) % CHUNK).astype(F32)
    same_head = _same_head_mask()
    first_lane = lax.broadcasted_iota(jnp.int32, (CHUNK, PAIR_W), 1) < HEAD_DIM
    zero = jnp.zeros((CHUNK, PAIR_W), BF16)
    power, inv = {}, {}
    for first in range(0, len(rows), GDN_GROUP):
        group = range(first, first + GDN_GROUP)
        for c in group:
            yield MXU
            rs = rows[c]
            for p, ls in enumerate(lanes):
                sc = _dot_nt(jnp.concatenate([k_b[rs, ls], q_b[rs, ls]], axis=0), _block_diag_pair(k_b[rs, ls]))
                power[c, p] = beta[rs, ls] * sc[0:CHUNK] * decay_strict[rs, ls]
                inv[c, p] = eye_pair - power[c, p]
                attn_ref[rs, ls] = (sc[CHUNK:] * decay_incl[rs, ls]).astype(BF16)
        for _ in range(int(math.log2(CHUNK)) - 1):
            for c in group:
                yield MXU
                for p in range(PAIRS):
                    power[c, p] = _pair_matmul(power[c, p], power[c, p])
            for c in group:
                yield MXU
                for p in range(PAIRS):
                    inv[c, p] = inv[c, p] + _pair_matmul(inv[c, p], power[c, p])
        for c in group:
            yield MXU
            rs = rows[c]
            for p, ls in enumerate(lanes):
                vb = vb_b[rs, ls]
                kbg = kbg_b[rs, ls]
                rhs = jnp.concatenate(
                    [jnp.concatenate([jnp.where(first_lane, vb, zero), jnp.where(first_lane, kbg, zero)], axis=1),
                     jnp.concatenate([jnp.where(first_lane, zero, vb), jnp.where(first_lane, zero, kbg)], axis=1)],
                    axis=0)
                uw_ref[c, p] = _dot(inv[c, p].astype(BF16), rhs).astype(BF16)
    for c, rs in enumerate(rows):
        yield MXU
        for p, ls in enumerate(lanes):
            uw = uw_ref[c, p]
            w_b = uw[:, PAIR_W:]
            kw_ku = _dot_tn(kd_b[rs, ls], jnp.concatenate([w_b, uw[:, 0:PAIR_W]], axis=1))
            aw_au = _dot(attn_ref[rs, ls], jnp.concatenate(
                [_block_diag_pair(w_b), _block_diag_pair(uw[:, 0:PAIR_W])], axis=1))
            kw = jnp.where(same_head, kw_ku[:, 0:PAIR_W], 0.0).astype(BF16)
            ku_ref[c, p] = jnp.where(same_head, kw_ku[:, PAIR_W:], 0.0)
            q_eff = (qd_b[rs, ls].astype(F32) - aw_au[:, 0:PAIR_W]).astype(BF16)
            kwq_ref[c, p] = jnp.concatenate([kw, q_eff], axis=0)
            od_ref[rs, ls] = aw_au[:, PAIR_W:]
    for c, (rs, gl) in enumerate(zip(rows, g_last)):
        yield MXU
        state = [st_ref[p] for p in range(PAIRS)]
        prod = [_dot(kwq_ref[c, p], st.astype(BF16)) for p, st in enumerate(state)]
        for p, ls in enumerate(lanes):
            st_ref[p] = state[p] * jnp.exp(gl[:, ls]) - prod[p][0:PAIR_W] + ku_ref[c, p]
            od_ref[rs, ls] = od_ref[rs, ls] + prod[p][PAIR_W:]
    yield MXU
    z = _dot(h, wqkvz_ref[:, qkv_w:])
    o_d = od_ref[...]
    ms = _seg_mean(o_d * o_d, seg)
    yield VPU
    return (o_d * lax.rsqrt(ms + NORM_EPS) * gnorm_ref[...] * _silu(z)).astype(BF16)


def _odd_kernel(x_ref, gain_ref,
                wsu_ref, bre_ref, bim_ref, pre_ref, pim_ref, cre_ref, cim_ref, dskip_ref, wglu_ref, bglu_ref,
                wqkvz_ref, wda_ref, wdb_ref, cw_ref, alog_ref, dtb_ref, gnorm_ref, seg_ref, tril_ref,
                onesbd_ref, woc_ref, wod_ref,
                o_ref,
                buf_ref, sst_ref, xs_ref, gst_ref, cs_ref, od_ref, attn_ref, uw_ref, ku_ref, kwq_ref):
    tile = x_ref.shape[1]

    @pl.when(pl.program_id(1) == 0)
    def _():
        sst_ref[...] = jnp.zeros_like(sst_ref)
        buf_ref[:, :, :, 0:SUBLANES, :] = jnp.zeros((3, 2, S5_LANE_BLOCKS, SUBLANES, LANES), F32)
        gst_ref[...] = jnp.zeros_like(gst_ref)
        cs_ref[:, 0:SUBLANES, :] = jnp.zeros((3 * MIX_W // LANES, SUBLANES, LANES), F32)

    x = x_ref[0]
    h = _rms_rows(x, gain_ref[...]).astype(BF16)
    o_c, o_d = _interleave(
        [_s5_steps(h, tile, wsu_ref, bre_ref, bim_ref, pre_ref, pim_ref, cre_ref, cim_ref,
                   dskip_ref, wglu_ref, bglu_ref, buf_ref, sst_ref, xs_ref),
         _gdn_steps(h, tile, wqkvz_ref, wda_ref, wdb_ref, cw_ref, alog_ref, dtb_ref, gnorm_ref, seg_ref,
                    tril_ref, onesbd_ref, gst_ref, cs_ref, od_ref, attn_ref, uw_ref, ku_ref, kwq_ref)],
        ODD_WEIGHTS)
    o_ref[0] = x + _dot(o_c, woc_ref[...]) + _dot(o_d, wod_ref[...])


def _odd_layer(x, gain, w_in, a_re, a_im, log_dt, b_re, b_im, c_re, c_im, d_skip, w_glu, b_glu,
               conv_w, a_log, dt_bias, gdn_norm, w_out):
    bsz, seq, _ = x.shape
    tile = min(MIX_TILE, seq)
    su1 = MIX_W
    z1 = su1 + 4 * MIX_W
    gs = (S5_GROUPS, S5_STATE)
    pow_re, pow_im, bb_re, bb_im = pl.pallas_call(
        _s5_prep_kernel,
        out_shape=[jax.ShapeDtypeStruct((SCAN_LEVELS,) + gs, F32)] * 2
        + [jax.ShapeDtypeStruct((S5_GROUP,) + gs, F32)] * 2,
        name="s5_discretize",
    )(a_re, a_im, log_dt.reshape(S5_GROUPS, 1), b_re.transpose(2, 0, 1), b_im.transpose(2, 0, 1))
    half_g = S5_GROUPS // 2
    eye = jnp.eye(half_g, dtype=F32)

    def in_blocks(bb):
        t = bb.reshape(S5_GROUP, 2, half_g, S5_STATE)
        return jnp.einsum('ckgp,gh->kgchp', t, eye).reshape(2, half_g * S5_GROUP, half_g * S5_STATE).astype(BF16)

    def out_blocks(cc):
        t = cc.reshape(2, half_g, S5_GROUP, S5_STATE)
        return jnp.einsum('kgop,gh->kgpho', t, eye).reshape(2, half_g * S5_STATE, half_g * S5_GROUP).astype(BF16)

    def lane_blocks(pw):
        return pw.reshape(SCAN_LEVELS, S5_LANE_BLOCKS, LANES).transpose(1, 0, 2)

    rep = lambda w: jnp.repeat(w, HEAD_DIM, axis=-1)
    args = (x, gain.reshape(1, D_MODEL),
            w_in[:, :su1].astype(BF16), in_blocks(bb_re), in_blocks(bb_im),
            lane_blocks(pow_re), lane_blocks(pow_im), out_blocks(c_re), out_blocks(c_im),
            d_skip.reshape(1, MIX_W), w_glu.astype(BF16), b_glu.reshape(1, MIX_W),
            w_in[:, su1:z1].astype(BF16), rep(w_in[:, z1:z1 + HEADS]).astype(BF16),
            rep(w_in[:, z1 + HEADS:]).astype(BF16), conv_w, rep(a_log).reshape(1, MIX_W),
            rep(dt_bias).reshape(1, MIX_W), _head_tile(gdn_norm, HEADS), _seg_matrix(),
            _block_tril(tile), _block_ones(tile), w_out[:MIX_W].astype(BF16), w_out[MIX_W:].astype(BF16))
    in_specs = [_tile_spec(tile, D_MODEL)] + [_const_spec(a.shape) for a in args[1:]]
    return pl.pallas_call(
        _odd_kernel,
        grid=(bsz, seq // tile),
        in_specs=in_specs,
        out_specs=_tile_spec(tile, D_MODEL),
        out_shape=jax.ShapeDtypeStruct(x.shape, F32),
        scratch_shapes=[pltpu.VMEM((3, 2, S5_LANE_BLOCKS, SUBLANES + tile, LANES), F32),
                        pltpu.VMEM((2, S5_LANE_BLOCKS, SUBLANES, LANES), F32),
                        pltpu.VMEM((2, S5_LANE_BLOCKS, tile, LANES), BF16),
                        pltpu.VMEM((PAIRS, PAIR_W, PAIR_W), F32),
                        pltpu.VMEM((3 * MIX_W // LANES, SUBLANES + tile, LANES), F32),
                        pltpu.VMEM((tile, MIX_W), F32),
                        pltpu.VMEM((tile, MIX_W), BF16),
                        pltpu.VMEM((tile // CHUNK, PAIRS, CHUNK, 2 * PAIR_W), BF16),
                        pltpu.VMEM((tile // CHUNK, PAIRS, PAIR_W, PAIR_W), F32),
                        pltpu.VMEM((tile // CHUNK, PAIRS, PAIR_W + CHUNK, PAIR_W), BF16)],
        compiler_params=_params(),
        name="s5_gdn_mixer",
    )(*args)


def kernel(x, norm_mix, norm_ffn, w_in_even, w_gla_gate, b_gla_gate, gla_out_norm, swa_q_norm, swa_k_norm, swa_sinks, w_out_even, w_in_odd, s5_a_re, s5_a_im, s5_log_dt, s5_b_re, s5_b_im, s5_c_re, s5_c_im, s5_d, s5_w_glu, s5_b_glu, gdn_conv_w, gdn_a_log, gdn_dt_bias, gdn_out_norm, w_out_odd, w_ffn_up, ffn_conv_w, ffn_conv_b, w_ffn_down):
    depth = norm_mix.shape[0]
    ffn_gains = norm_ffn.reshape(depth, 1, D_MODEL)
    ffn_up = w_ffn_up.astype(BF16)
    ffn_down = w_ffn_down.astype(BF16)
    ffn_bias = ffn_conv_b.reshape(depth, 1, 2 * FFN_DIM)
    for layer in range(depth):
        i = layer // 2
        if layer % 2 == 0:
            x = _even_layer(x, norm_mix[layer], w_in_even[i], w_gla_gate[i], b_gla_gate[i],
                            gla_out_norm[i], swa_q_norm[i], swa_k_norm[i], swa_sinks[i], w_out_even[i])
        else:
            x = _odd_layer(x, norm_mix[layer], w_in_odd[i], s5_a_re[i], s5_a_im[i], s5_log_dt[i],
                           s5_b_re[i], s5_b_im[i], s5_c_re[i], s5_c_im[i], s5_d[i], s5_w_glu[i],
                           s5_b_glu[i], gdn_conv_w[i], gdn_a_log[i], gdn_dt_bias[i], gdn_out_norm[i],
                           w_out_odd[i])
        x = _ffn_layer(x, layer, ffn_gains, ffn_up, ffn_conv_w, ffn_bias, ffn_down)
    return x
```

```python
import math

import jax
import jax.numpy as jnp
from jax import lax
from jax.experimental import pallas as pl
from jax.experimental.pallas import tpu as pltpu

F32 = jnp.float32
BF16 = jnp.bfloat16

D_MODEL = 1024
HEAD_DIM = 64
HEADS = 8
MIX_W = HEADS * HEAD_DIM
PAIRS = HEADS // 2
PAIR_W = 2 * HEAD_DIM
GLA_RANK = 16
GLA_GATE_NORM = 16.0
CHUNK = 64
SWA_BLOCK = 128
SWA_KV_HEADS = 2
S5_GROUP = 16
S5_GROUPS = 32
S5_STATE = 64
S5_LANES = S5_GROUPS * S5_STATE
GDN_CONV = 4
FFN_DIM = 2816
FFN_CONV = 3
NORM_EPS = 1e-6
LANES = 128
SUBLANES = 8
MXU_COLS = 256
VMEM_LIMIT = 56 * 1024 * 1024
S5_LANE_BLOCKS = S5_LANES // LANES

MIX_TILE = 256
FFN_TILE = 512
FFN_COLS = MXU_COLS
SCAN_LEVELS = int(math.log2(SUBLANES)) + 1
SCAN_UNROLL = 8
MXU = "mxu"
VPU = "vpu"
EVEN_WEIGHTS = (1, 2)
ODD_WEIGHTS = (1, 1)
GDN_GROUP = 4


def _dot(a, b):
    return jnp.dot(a, b, preferred_element_type=F32)


def _dot_nt(a, b):
    return lax.dot_general(a, b, (((1,), (1,)), ((), ())), preferred_element_type=F32)


def _dot_tn(a, b):
    return lax.dot_general(a, b, (((0,), (0,)), ((), ())), preferred_element_type=F32)


def _dot01_lhs(m01, x):
    hi = x.astype(BF16)
    lo = (x - hi.astype(F32)).astype(BF16)
    return _dot(m01, hi) + _dot(m01, lo)


def _sigmoid(x):
    return 1.0 / (1.0 + jnp.exp(-x))


def _silu(x):
    return x * _sigmoid(x)


def _softplus(x):
    return jnp.maximum(x, 0.0) + jnp.log1p(jnp.exp(-jnp.abs(x)))


def _rms_rows(x, gain_row):
    ms = jnp.mean(x * x, axis=-1, keepdims=True)
    return x * lax.rsqrt(ms + NORM_EPS) * gain_row


def _seg_mean(y, seg01):
    return _dot(y.astype(BF16), seg01) * (1.0 / HEAD_DIM)


def _block_diag_pair(v):
    first = lax.broadcasted_iota(jnp.int32, v.shape, 1) < HEAD_DIM
    zero = jnp.zeros_like(v)
    return jnp.concatenate([jnp.where(first, v, zero), jnp.where(first, zero, v)], axis=0)


def _same_head_mask():
    r = lax.broadcasted_iota(jnp.int32, (PAIR_W, PAIR_W), 0)
    c = lax.broadcasted_iota(jnp.int32, (PAIR_W, PAIR_W), 1)
    return (r < HEAD_DIM) == (c < HEAD_DIM)


def _causal_conv(u_ref, j, taps_ref, lanes, tile, acc=None):
    taps = taps_ref.shape[0]
    for k in range(taps):
        off = SUBLANES - (taps - 1) + k
        term = taps_ref[k:k + 1, lanes] * u_ref[j, off:off + tile, :]
        acc = term if acc is None else acc + term
    return acc


def _interleave(streams, weights):
    results = [None] * len(streams)
    live = list(range(len(streams)))
    while live:
        for i in list(live):
            for _ in range(weights[i]):
                try:
                    next(streams[i])
                except StopIteration as stop:
                    results[i] = stop.value
                    live.remove(i)
                    break
    return results


def _const_spec(shape):
    nd = len(shape)
    return pl.BlockSpec(shape, lambda b, l: (0,) * nd, pipeline_mode=pl.Buffered(1))


def _tile_spec(tile, width):
    return pl.BlockSpec((1, tile, width), lambda b, l: (b, l, 0))


def _params():
    return pltpu.CompilerParams(dimension_semantics=("arbitrary", "arbitrary"),
                                vmem_limit_bytes=VMEM_LIMIT)


def _ffn_kernel(x_ref, gain_ref, wa_ref, wg_ref, cwa_ref, cwg_ref, cba_ref, cbg_ref, wd_ref,
                o_ref, ua_ref, ug_ref, act_ref):
    tile = x_ref.shape[1]
    blocks = FFN_COLS // LANES

    @pl.when(pl.program_id(1) == 0)
    def _():
        ua_ref[:, 0:SUBLANES, :] = jnp.zeros((FFN_DIM // LANES, SUBLANES, LANES), F32)
        ug_ref[:, 0:SUBLANES, :] = jnp.zeros((FFN_DIM // LANES, SUBLANES, LANES), F32)

    x = x_ref[0]
    h = _rms_rows(x, gain_ref[...]).astype(BF16)
    for c in range(FFN_DIM // FFN_COLS):
        sl = slice(c * FFN_COLS, (c + 1) * FFN_COLS)
        ua = _dot(h, wa_ref[:, sl])
        ug = _dot(h, wg_ref[:, sl])
        for t in range(blocks):
            j = c * blocks + t
            lanes = slice(j * LANES, (j + 1) * LANES)
            ua_ref[j, SUBLANES:SUBLANES + tile, :] = ua[:, t * LANES:(t + 1) * LANES]
            ug_ref[j, SUBLANES:SUBLANES + tile, :] = ug[:, t * LANES:(t + 1) * LANES]
            a = _causal_conv(ua_ref, j, cwa_ref, lanes, tile, cba_ref[:, lanes])
            g = _causal_conv(ug_ref, j, cwg_ref, lanes, tile, cbg_ref[:, lanes])
            act_ref[:, lanes] = (_silu(g) * a).astype(BF16)
    ua_ref[:, 0:SUBLANES, :] = ua_ref[:, tile:tile + SUBLANES, :]
    ug_ref[:, 0:SUBLANES, :] = ug_ref[:, tile:tile + SUBLANES, :]
    o_ref[0] = x + _dot(act_ref[...], wd_ref[...])


def _layer_spec(rows, cols, layer, col_block=0):
    return pl.BlockSpec((None, rows, cols), lambda b, l: (layer, 0, col_block),
                        pipeline_mode=pl.Buffered(1))


def _ffn_layer(x, layer, gains, w_up, conv_w, conv_b, w_down):
    bsz, seq, _ = x.shape
    tile = min(FFN_TILE, seq)
    args = (x, gains, w_up, w_up, conv_w, conv_w, conv_b, conv_b, w_down)
    in_specs = [_tile_spec(tile, D_MODEL),
                _layer_spec(1, D_MODEL, layer),
                _layer_spec(D_MODEL, FFN_DIM, layer, 0), _layer_spec(D_MODEL, FFN_DIM, layer, 1),
                _layer_spec(FFN_CONV, FFN_DIM, layer, 0), _layer_spec(FFN_CONV, FFN_DIM, layer, 1),
                _layer_spec(1, FFN_DIM, layer, 0), _layer_spec(1, FFN_DIM, layer, 1),
                _layer_spec(FFN_DIM, D_MODEL, layer)]
    return pl.pallas_call(
        _ffn_kernel,
        grid=(bsz, seq // tile),
        in_specs=in_specs,
        out_specs=_tile_spec(tile, D_MODEL),
        out_shape=jax.ShapeDtypeStruct(x.shape, F32),
        scratch_shapes=[pltpu.VMEM((FFN_DIM // LANES, SUBLANES + tile, LANES), F32),
                        pltpu.VMEM((FFN_DIM // LANES, SUBLANES + tile, LANES), F32),
                        pltpu.VMEM((tile, FFN_DIM), BF16)],
        compiler_params=_params(),
        name="conv_ffn",
    )(*args)


def _gla_steps(h, tile, wmain_ref, wglr_ref, wgate_ref, bgate_ref, glan_ref, seg_ref, tril_ref,
               st_ref, oa_ref):
    yield MXU
    proj = _dot(h, wmain_ref[:, 0:4 * MIX_W])
    gq = proj[:, 0 * MIX_W:1 * MIX_W]
    gk = proj[:, 1 * MIX_W:2 * MIX_W]
    gv = proj[:, 2 * MIX_W:3 * MIX_W]
    gr = proj[:, 3 * MIX_W:4 * MIX_W]
    glr = _dot(h, wglr_ref[...])
    gate = _dot(glr.astype(BF16), wgate_ref[...]) + bgate_ref[...]
    yield VPU
    log_g = (jnp.minimum(gate, 0.0) - jnp.log1p(jnp.exp(-jnp.abs(gate)))) * (1.0 / GLA_GATE_NORM)
    yield MXU
    cum = _dot01_lhs(tril_ref[...], log_g)
    yield VPU
    q_pos = (gq * HEAD_DIM ** -0.5) * jnp.exp(cum)
    k_neg = gk * jnp.exp(-cum)
    row = lax.broadcasted_iota(jnp.int32, (CHUNK, PAIR_W), 0)
    lane = lax.broadcasted_iota(jnp.int32, (CHUNK, PAIR_W), 1)
    causal = row >= (lane % CHUNK)
    same_head = _same_head_mask()
    rows = [slice(c * CHUNK, (c + 1) * CHUNK) for c in range(tile // CHUNK)]
    lanes = [slice(p * PAIR_W, (p + 1) * PAIR_W) for p in range(PAIRS)]
    cum_last = [cum[rs][CHUNK - 1:CHUNK] for rs in rows]
    k_dec = jnp.concatenate([gk[rs] * jnp.exp(cl - cum[rs]) for rs, cl in zip(rows, cum_last)], axis=0)
    q_b = q_pos.astype(BF16)
    kn_b = k_neg.astype(BF16)
    kd_b = k_dec.astype(BF16)
    v_b = gv.astype(BF16)
    scores, o_intra, upd = {}, {}, {}
    for c, rs in enumerate(rows):
        yield MXU
        for p, ls in enumerate(lanes):
            s = _dot_nt(q_b[rs, ls], _block_diag_pair(kn_b[rs, ls]))
            scores[c, p] = jnp.where(causal, s, 0.0).astype(BF16)
    for c, rs in enumerate(rows):
        yield MXU
        for p, ls in enumerate(lanes):
            upd[c, p] = jnp.where(same_head, _dot_tn(v_b[rs, ls], kd_b[rs, ls]), 0.0)
    for c, rs in enumerate(rows):
        yield MXU
        for p, ls in enumerate(lanes):
            o_intra[c, p] = _dot(scores[c, p], _block_diag_pair(v_b[rs, ls]))
    yield VPU
    states = {}
    for p, ls in enumerate(lanes):
        state_t = st_ref[p]
        for c, cl in enumerate(cum_last):
            states[c, p] = state_t
            state_t = state_t * jnp.exp(cl[:, ls]) + upd[c, p]
        st_ref[p] = state_t
    for c, rs in enumerate(rows):
        yield MXU
        for p, ls in enumerate(lanes):
            oa_ref[rs, ls] = o_intra[c, p] + _dot_nt(q_b[rs, ls], states[c, p].astype(BF16))
    yield MXU
    o_a = oa_ref[...]
    ms = _seg_mean(o_a * o_a, seg_ref[...])
    yield VPU
    return (o_a * lax.rsqrt(ms + NORM_EPS) * glan_ref[...] * _silu(gr)).astype(BF16)


def _swa_steps(h, tile, seq_tile, sinks_ref, wmain_ref, wkv_ref, qn_ref, kn_ref, seg_ref, bias_ref,
               kvs_ref, ob_ref):
    yield MXU
    sq = _dot(h, wmain_ref[:, 4 * MIX_W:5 * MIX_W])
    kv = _dot(h, wkv_ref[...])
    seg = seg_ref[...]
    yield MXU
    ms_q = _seg_mean(sq * sq, seg)
    sk = kv[:, 0:PAIR_W]
    ms_k = _seg_mean(sk * sk, seg[0:PAIR_W, 0:PAIR_W])
    yield VPU
    qn = (sq * lax.rsqrt(ms_q + NORM_EPS) * qn_ref[...] * HEAD_DIM ** -0.5).astype(BF16)
    kvs_ref[SWA_BLOCK:SWA_BLOCK + tile, 0:PAIR_W] = sk * lax.rsqrt(ms_k + NORM_EPS) * kn_ref[...]
    kvs_ref[SWA_BLOCK:SWA_BLOCK + tile, PAIR_W:2 * PAIR_W] = kv[:, PAIR_W:2 * PAIR_W]
    skey = lax.broadcasted_iota(jnp.int32, (2 * SWA_BLOCK, 2 * SWA_BLOCK), 1)
    top_col = lax.broadcasted_iota(jnp.int32, (2 * SWA_BLOCK, 1), 0) < SWA_BLOCK
    ones_cols = jnp.ones((2 * SWA_BLOCK, PAIR_W), BF16)
    first_lane = lax.broadcasted_iota(jnp.int32, (2 * SWA_BLOCK, PAIR_W), 1) < HEAD_DIM
    no_prev = jnp.where(skey < SWA_BLOCK, jnp.where(seq_tile > 0, 0.0, -jnp.inf), 0.0)
    for j in range(tile // SWA_BLOCK):
        ks = slice(j * SWA_BLOCK, (j + 2) * SWA_BLOCK)
        keys = kvs_ref[ks, 0:PAIR_W].astype(BF16)
        vals = kvs_ref[ks, PAIR_W:2 * PAIR_W].astype(BF16)
        zero_v = jnp.zeros_like(vals)
        vals0 = jnp.concatenate([jnp.where(first_lane, vals, zero_v), ones_cols], axis=1)
        vals1 = jnp.concatenate([jnp.where(first_lane, zero_v, vals), ones_cols], axis=1)
        for p in range(PAIRS):
            yield MXU
            qs = _block_diag_pair(qn[j * SWA_BLOCK:(j + 1) * SWA_BLOCK, p * PAIR_W:(p + 1) * PAIR_W])
            s = _dot_nt(qs, keys)
            yield VPU
            s = s + bias_ref[p]
            if j == 0:
                s = s + no_prev
            sink = jnp.where(top_col, sinks_ref[p], sinks_ref[p + PAIRS])
            m = jnp.maximum(jnp.max(s, axis=-1, keepdims=True), sink)
            pe = jnp.exp(s - m).astype(BF16)
            sink_e = jnp.exp(sink - m)
            yield MXU
            r0 = _dot(pe[0:SWA_BLOCK], vals0)
            r1 = _dot(pe[SWA_BLOCK:], vals1)
            o_pair = (r0[:, 0:PAIR_W] / (r0[:, PAIR_W:] + sink_e[0:SWA_BLOCK])
                      + r1[:, 0:PAIR_W] / (r1[:, PAIR_W:] + sink_e[SWA_BLOCK:]))
            ob_ref[j * SWA_BLOCK:(j + 1) * SWA_BLOCK, p * PAIR_W:(p + 1) * PAIR_W] = o_pair.astype(BF16)
    kvs_ref[0:SWA_BLOCK, :] = kvs_ref[tile:tile + SWA_BLOCK, :]
    return ob_ref[...]


def _even_kernel(sinks_ref, x_ref, gain_ref, wmain_ref, wkv_ref, wglr_ref, wgate_ref, bgate_ref,
                 glan_ref, qn_ref, kn_ref, seg_ref, tril_ref, bias_ref, woa_ref, wob_ref,
                 o_ref, st_ref, kvs_ref, oa_ref, ob_ref):
    tile = x_ref.shape[1]
    seq_tile = pl.program_id(1)

    @pl.when(seq_tile == 0)
    def _():
        st_ref[...] = jnp.zeros_like(st_ref)
        kvs_ref[0:SWA_BLOCK, :] = jnp.zeros((SWA_BLOCK, 2 * PAIR_W), F32)

    x = x_ref[0]
    h = _rms_rows(x, gain_ref[...]).astype(BF16)
    o_a, o_b = _interleave(
        [_gla_steps(h, tile, wmain_ref, wglr_ref, wgate_ref, bgate_ref, glan_ref, seg_ref, tril_ref,
                    st_ref, oa_ref),
         _swa_steps(h, tile, seq_tile, sinks_ref, wmain_ref, wkv_ref, qn_ref, kn_ref, seg_ref, bias_ref,
                    kvs_ref, ob_ref)],
        EVEN_WEIGHTS)
    o_ref[0] = x + _dot(o_a, woa_ref[...]) + _dot(o_b, wob_ref[...])


def _block_tril(tile):
    r = jnp.arange(tile)
    same = (r[:, None] // CHUNK) == (r[None, :] // CHUNK)
    return (same & (r[:, None] >= r[None, :])).astype(BF16)


def _block_ones(tile):
    r = jnp.arange(tile)
    return ((r[:, None] // CHUNK) == (r[None, :] // CHUNK)).astype(BF16)


def _seg_matrix():
    r = jnp.arange(MIX_W)
    return ((r[:, None] // HEAD_DIM) == (r[None, :] // HEAD_DIM)).astype(BF16)


def _swa_bias():
    r = jnp.arange(2 * SWA_BLOCK)
    dist = (r[:, None] % SWA_BLOCK) + SWA_BLOCK - r[None, :]
    valid = (dist >= 0) & (dist < SWA_BLOCK)
    head = jnp.arange(PAIRS)[:, None, None] + jnp.where(r < SWA_BLOCK, 0, PAIRS)[None, :, None]
    slope = jnp.exp2(-(head + 1).astype(F32))
    return jnp.where(valid[None], -slope * dist[None].astype(F32), -jnp.inf)


def _head_tile(v, reps):
    return jnp.tile(v.astype(F32), reps).reshape(1, reps * v.shape[0])


def _even_layer(x, gain, w_in, w_gate, b_gate, gla_norm, q_norm, k_norm, sinks, w_out):
    bsz, seq, _ = x.shape
    tile = min(2 * MIX_TILE, seq)
    order = jnp.array([h for p in range(PAIRS) for h in (p, p + PAIRS)])
    c0 = 4 * MIX_W
    glr0 = c0
    sq0 = c0 + GLA_RANK
    sk0 = sq0 + MIX_W
    w_sq = w_in[:, sq0:sk0].reshape(D_MODEL, HEADS, HEAD_DIM)[:, order].reshape(D_MODEL, MIX_W)
    w_main = jnp.concatenate([w_in[:, :c0], w_sq], axis=1).astype(BF16)
    w_kv = w_in[:, sk0:].astype(BF16)
    w_glr = jnp.pad(w_in[:, glr0:sq0], ((0, 0), (0, LANES - GLA_RANK))).astype(BF16)
    w_gate_p = jnp.pad(w_gate, ((0, LANES - GLA_RANK), (0, 0))).astype(BF16)
    w_out_a = w_out[:MIX_W].astype(BF16)
    w_out_b = w_out[MIX_W:].reshape(HEADS, HEAD_DIM, D_MODEL)[order].reshape(MIX_W, D_MODEL).astype(BF16)
    args = (x, gain.reshape(1, D_MODEL), w_main, w_kv, w_glr, w_gate_p, b_gate.reshape(1, MIX_W),
            _head_tile(gla_norm, HEADS), _head_tile(q_norm, HEADS), _head_tile(k_norm, SWA_KV_HEADS),
            _seg_matrix(), _block_tril(tile), _swa_bias(), w_out_a, w_out_b)
    in_specs = ([pl.BlockSpec(memory_space=pltpu.SMEM), _tile_spec(tile, D_MODEL)]
                + [_const_spec(a.shape) for a in args[1:]])
    return pl.pallas_call(
        _even_kernel,
        grid=(bsz, seq // tile),
        in_specs=in_specs,
        out_specs=_tile_spec(tile, D_MODEL),
        out_shape=jax.ShapeDtypeStruct(x.shape, F32),
        scratch_shapes=[pltpu.VMEM((PAIRS, PAIR_W, PAIR_W), F32),
                        pltpu.VMEM((SWA_BLOCK + tile, 2 * PAIR_W), F32),
                        pltpu.VMEM((tile, MIX_W), F32),
                        pltpu.VMEM((tile, MIX_W), BF16)],
        compiler_params=_params(),
        name="gla_swa_mixer",
    )(sinks.astype(F32), *args)


def _s5_prep_kernel(are_ref, aim_ref, ldt_ref, bre_ref, bim_ref, pre_ref, pim_ref, bbre_ref, bbim_ref):
    a_re = are_ref[...]
    a_im = aim_ref[...]
    dt = jnp.exp(ldt_ref[...])
    for k in range(SCAN_LEVELS):
        step = float(2 ** k)
        mag = jnp.exp(step * dt * a_re)
        ang = step * dt * a_im
        pre_ref[k] = mag * jnp.cos(ang)
        pim_ref[k] = mag * jnp.sin(ang)
    abar_re = pre_ref[0]
    abar_im = pim_ref[0]
    den = a_re * a_re + a_im * a_im
    f_re = ((abar_re - 1.0) * a_re + abar_im * a_im) / den
    f_im = (abar_im * a_re - (abar_re - 1.0) * a_im) / den
    for c in range(S5_GROUP):
        bbre_ref[c] = f_re * bre_ref[c] - f_im * bim_ref[c]
        bbim_ref[c] = f_re * bim_ref[c] + f_im * bre_ref[c]


def _s5_steps(h, tile, wsu_ref, bre_ref, bim_ref, pre_ref, pim_ref, cre_ref, cim_ref,
              dskip_ref, wglu_ref, bglu_ref, buf_ref, st_ref, xs_ref):
    half_in = MIX_W // 2
    half_blocks = S5_LANE_BLOCKS // 2
    body = slice(SUBLANES, SUBLANES + tile)
    tail = slice(tile, tile + SUBLANES)
    head = slice(0, SUBLANES)
    yield MXU
    u = _dot(h, wsu_ref[...])
    ub = u.astype(BF16)
    for kt in range(2):
        yield MXU
        u_half = ub[:, kt * half_in:(kt + 1) * half_in]
        bu_re = _dot(u_half, bre_ref[kt])
        bu_im = _dot(u_half, bim_ref[kt])
        for j in range(half_blocks):
            lb = kt * half_blocks + j
            buf_ref[0, 0, lb, body, :] = bu_re[:, j * LANES:(j + 1) * LANES]
            buf_ref[0, 1, lb, body, :] = bu_im[:, j * LANES:(j + 1) * LANES]

    def scan_blocks(blocks):
        for lb in blocks:
            yield VPU
            p_re = pre_ref[lb]
            p_im = pim_ref[lb]
            for k, (src, dst) in enumerate(((0, 1), (1, 2), (2, 1))):
                shift = 2 ** k
                shifted = slice(SUBLANES - shift, SUBLANES - shift + tile)
                cur_re = buf_ref[src, 0, lb, body, :]
                cur_im = buf_ref[src, 1, lb, body, :]
                sh_re = buf_ref[src, 0, lb, shifted, :]
                sh_im = buf_ref[src, 1, lb, shifted, :]
                a_re = p_re[k:k + 1]
                a_im = p_im[k:k + 1]
                new_re = cur_re + a_re * sh_re - a_im * sh_im
                new_im = cur_im + a_re * sh_im + a_im * sh_re
                buf_ref[src, 0, lb, head, :] = buf_ref[src, 0, lb, tail, :]
                buf_ref[src, 1, lb, head, :] = buf_ref[src, 1, lb, tail, :]
                buf_ref[dst, 0, lb, body, :] = new_re
                buf_ref[dst, 1, lb, body, :] = new_im
        a8 = [(pre_ref[lb][3:4], pim_ref[lb][3:4]) for lb in blocks]
        state = [(st_ref[0, lb], st_ref[1, lb]) for lb in blocks]
        groups = tile // SUBLANES
        for g in range(groups):
            if g % (groups // 4) == 0:
                yield VPU
            grp = slice(SUBLANES * (g + 1), SUBLANES * (g + 2))
            for n, lb in enumerate(blocks):
                a_re, a_im = a8[n]
                s_re, s_im = state[n]
                x_re = buf_ref[1, 0, lb, grp, :] + a_re * s_re - a_im * s_im
                x_im = buf_ref[1, 1, lb, grp, :] + a_re * s_im + a_im * s_re
                buf_ref[1, 0, lb, grp, :] = x_re
                buf_ref[1, 1, lb, grp, :] = x_im
                state[n] = (x_re, x_im)
        yield VPU
        for n, lb in enumerate(blocks):
            st_ref[0, lb] = state[n][0]
            st_ref[1, lb] = state[n][1]
            xs_ref[0, lb] = buf_ref[1, 0, lb, body, :].astype(BF16)
            xs_ref[1, lb] = buf_ref[1, 1, lb, body, :].astype(BF16)

    ys = []
    for kt in range(2):
        blocks = list(range(kt * half_blocks, (kt + 1) * half_blocks))
        for n in range(0, half_blocks, SCAN_UNROLL):
            yield from scan_blocks(blocks[n:n + SCAN_UNROLL])
        yield MXU
        x_re = jnp.concatenate([xs_ref[0, lb] for lb in blocks], axis=1)
        x_im = jnp.concatenate([xs_ref[1, lb] for lb in blocks], axis=1)
        ys.append(_dot(x_re, cre_ref[kt]) - _dot(x_im, cim_ref[kt]))
    yield VPU
    y = jnp.concatenate(ys, axis=1) + dskip_ref[...] * u
    y = 0.5 * y * (1.0 + jnp.tanh(math.sqrt(2.0 / math.pi) * (y + 0.044715 * (y * y * y))))
    yield MXU
    gate = _dot(y.astype(BF16), wglu_ref[...]) + bglu_ref[...]
    yield VPU
    return (y * _sigmoid(gate)).astype(BF16)


def _gdn_steps(h, tile, wqkvz_ref, wda_ref, wdb_ref, cw_ref, alog_ref, dtb_ref, gnorm_ref, seg_ref,
               tril_ref, onesbd_ref, st_ref, cs_ref, od_ref, attn_ref, uw_ref, ku_ref, kwq_ref):
    qkv_w = 3 * MIX_W
    yield MXU
    da = _dot(h, wda_ref[...])
    db = _dot(h, wdb_ref[...])
    yield VPU
    log_alpha = -jnp.exp(alog_ref[...]) * _softplus(da + dtb_ref[...])
    beta = _sigmoid(db)
    row = lax.broadcasted_iota(jnp.int32, (tile, MIX_W), 0) % CHUNK
    key = lax.broadcasted_iota(jnp.int32, (tile, MIX_W), 1) % CHUNK
    yield MXU
    g = _dot01_lhs(tril_ref[...], log_alpha)
    yield MXU
    g_key = _dot01_lhs(onesbd_ref[...], jnp.where(row <= key, log_alpha, 0.0))
    yield MXU
    proj = _dot(h, wqkvz_ref[:, 0:qkv_w])
    yield VPU
    decay_incl = jnp.exp(jnp.where(row >= key, g - g_key, -jnp.inf))
    decay_strict = jnp.where(row > key, decay_incl, 0.0)
    exp_g = jnp.exp(g)
    conv = []
    for j in range(qkv_w // LANES):
        if j % 4 == 0:
            yield VPU
        lanes_j = slice(j * LANES, (j + 1) * LANES)
        cs_ref[j, SUBLANES:SUBLANES + tile, :] = proj[:, lanes_j]
        conv.append(_silu(_causal_conv(cs_ref, j, cw_ref, lanes_j, tile)))
    cs_ref[:, 0:SUBLANES, :] = cs_ref[:, tile:tile + SUBLANES, :]
    qkv = jnp.concatenate(conv, axis=1)
    q = qkv[:, 0:MIX_W]
    k_ = qkv[:, MIX_W:2 * MIX_W]
    v = qkv[:, 2 * MIX_W:]
    yield MXU
    seg = seg_ref[...]
    ss_q = _seg_mean(q * q, seg) * HEAD_DIM
    ss_k = _seg_mean(k_ * k_, seg) * HEAD_DIM
    z = _dot(h, wqkvz_ref[:, qkv_w:])
    yield VPU
    q = q * lax.rsqrt(ss_q + NORM_EPS) * HEAD_DIM ** -0.5
    k_ = k_ * lax.rsqrt(ss_k + NORM_EPS)
    rows = [slice(c * CHUNK, (c + 1) * CHUNK) for c in range(tile // CHUNK)]
    lanes = [slice(p * PAIR_W, (p + 1) * PAIR_W) for p in range(PAIRS)]
    g_last = [g[rs][CHUNK - 1:CHUNK] for rs in rows]
    k_dec = jnp.concatenate([k_[rs] * jnp.exp(gl - g[rs]) for rs, gl in zip(rows, g_last)], axis=0)
    k_b = k_.astype(BF16)
    q_b = q.astype(BF16)
    kd_b = k_dec.astype(BF16)
    qd_b = (q * exp_g).astype(BF16)
    vb_b = (v * beta).astype(BF16)
    kbg_b = (k_ * beta * exp_g).astype(BF16)
    eye_pair = (lax.broadcasted_iota(jnp.int32, (CHUNK, PAIR_W), 0)
                == lax.broadcasted_iota(jnp.int32, (CHUNK, PAIR_W), 1) % CHUNK).astype(F32)
    same_head = _same_head_mask()
    first_lane = lax.broadcasted_iota(jnp.int32, (CHUNK, PAIR_W), 1) < HEAD_DIM
    zero = jnp.zeros((CHUNK, PAIR_W), BF16)
    power, power_bd, inv = {}, {}, {}
    for first in range(0, len(rows), GDN_GROUP):
        group = range(first, first + GDN_GROUP)
        for c in group:
            yield MXU
            rs = rows[c]
            for p, ls in enumerate(lanes):
                sc = _dot_nt(jnp.concatenate([k_b[rs, ls], q_b[rs, ls]], axis=0), _block_diag_pair(k_b[rs, ls]))
                lower = beta[rs, ls] * sc[0:CHUNK] * decay_strict[rs, ls]
                power[c, p] = lower.astype(BF16)
                power_bd[c, p] = _block_diag_pair(power[c, p])
                inv[c, p] = eye_pair - lower
                attn_ref[rs, ls] = (sc[CHUNK:] * decay_incl[rs, ls]).astype(BF16)
        for _ in range(int(math.log2(CHUNK)) - 1):
            for c in group:
                yield MXU
                for p in range(PAIRS):
                    power[c, p] = _dot(power[c, p], power_bd[c, p]).astype(BF16)
                    power_bd[c, p] = _block_diag_pair(power[c, p])
            for c in group:
                yield MXU
                for p in range(PAIRS):
                    inv[c, p] = inv[c, p] + _dot(inv[c, p].astype(BF16), power_bd[c, p])
        for c in group:
            yield MXU
            rs = rows[c]
            for p, ls in enumerate(lanes):
                vb = vb_b[rs, ls]
                kbg = kbg_b[rs, ls]
                rhs = jnp.concatenate(
                    [jnp.concatenate([jnp.where(first_lane, vb, zero), jnp.where(first_lane, kbg, zero)], axis=1),
                     jnp.concatenate([jnp.where(first_lane, zero, vb), jnp.where(first_lane, zero, kbg)], axis=1)],
                    axis=0)
                uw_ref[c, p] = _dot(inv[c, p].astype(BF16), rhs).astype(BF16)
    for c, rs in enumerate(rows):
        yield MXU
        for p, ls in enumerate(lanes):
            uw = uw_ref[c, p]
            w_b = uw[:, PAIR_W:]
            kw_ku = _dot_tn(kd_b[rs, ls], jnp.concatenate([w_b, uw[:, 0:PAIR_W]], axis=1))
            aw_au = _dot(attn_ref[rs, ls], jnp.concatenate(
                [_block_diag_pair(w_b), _block_diag_pair(uw[:, 0:PAIR_W])], axis=1))
            kw = jnp.where(same_head, kw_ku[:, 0:PAIR_W], 0.0).astype(BF16)
            ku_ref[c, p] = jnp.where(same_head, kw_ku[:, PAIR_W:], 0.0)
            q_eff = (qd_b[rs, ls].astype(F32) - aw_au[:, 0:PAIR_W]).astype(BF16)
            kwq_ref[c, p] = jnp.concatenate([kw, q_eff], axis=0)
            od_ref[rs, ls] = aw_au[:, PAIR_W:]
    for c, (rs, gl) in enumerate(zip(rows, g_last)):
        yield MXU
        state = [st_ref[p] for p in range(PAIRS)]
        prod = [_dot(kwq_ref[c, p], st.astype(BF16)) for p, st in enumerate(state)]
        for p, ls in enumerate(lanes):
            st_ref[p] = state[p] * jnp.exp(gl[:, ls]) - prod[p][0:PAIR_W] + ku_ref[c, p]
            od_ref[rs, ls] = od_ref[rs, ls] + prod[p][PAIR_W:]
    yield MXU
    o_d = od_ref[...]
    ms = _seg_mean(o_d * o_d, seg)
    yield VPU
    return (o_d * lax.rsqrt(ms + NORM_EPS) * gnorm_ref[...] * _silu(z)).astype(BF16)


def _odd_kernel(x_ref, gain_ref,
                wsu_ref, bre_ref, bim_ref, pre_ref, pim_ref, cre_ref, cim_ref, dskip_ref, wglu_ref, bglu_ref,
                wqkvz_ref, wda_ref, wdb_ref, cw_ref, alog_ref, dtb_ref, gnorm_ref, seg_ref, tril_ref,
                onesbd_ref, woc_ref, wod_ref,
                o_ref,
                buf_ref, sst_ref, xs_ref, gst_ref, cs_ref, od_ref, attn_ref, uw_ref, ku_ref, kwq_ref):
    tile = x_ref.shape[1]

    @pl.when(pl.program_id(1) == 0)
    def _():
        sst_ref[...] = jnp.zeros_like(sst_ref)
        buf_ref[:, :, :, 0:SUBLANES, :] = jnp.zeros((3, 2, S5_LANE_BLOCKS, SUBLANES, LANES), F32)
        gst_ref[...] = jnp.zeros_like(gst_ref)
        cs_ref[:, 0:SUBLANES, :] = jnp.zeros((3 * MIX_W // LANES, SUBLANES, LANES), F32)

    x = x_ref[0]
    h = _rms_rows(x, gain_ref[...]).astype(BF16)
    o_c, o_d = _interleave(
        [_s5_steps(h, tile, wsu_ref, bre_ref, bim_ref, pre_ref, pim_ref, cre_ref, cim_ref,
                   dskip_ref, wglu_ref, bglu_ref, buf_ref, sst_ref, xs_ref),
         _gdn_steps(h, tile, wqkvz_ref, wda_ref, wdb_ref, cw_ref, alog_ref, dtb_ref, gnorm_ref, seg_ref,
                    tril_ref, onesbd_ref, gst_ref, cs_ref, od_ref, attn_ref, uw_ref, ku_ref, kwq_ref)],
        ODD_WEIGHTS)
    o_ref[0] = x + _dot(o_c, woc_ref[...]) + _dot(o_d, wod_ref[...])


def _odd_layer(x, gain, w_in, a_re, a_im, log_dt, b_re, b_im, c_re, c_im, d_skip, w_glu, b_glu,
               conv_w, a_log, dt_bias, gdn_norm, w_out):
    bsz, seq, _ = x.shape
    tile = min(MIX_TILE, seq)
    su1 = MIX_W
    z1 = su1 + 4 * MIX_W
    gs = (S5_GROUPS, S5_STATE)
    pow_re, pow_im, bb_re, bb_im = pl.pallas_call(
        _s5_prep_kernel,
        out_shape=[jax.ShapeDtypeStruct((SCAN_LEVELS,) + gs, F32)] * 2
        + [jax.ShapeDtypeStruct((S5_GROUP,) + gs, F32)] * 2,
        name="s5_discretize",
    )(a_re, a_im, log_dt.reshape(S5_GROUPS, 1), b_re.transpose(2, 0, 1), b_im.transpose(2, 0, 1))
    half_g = S5_GROUPS // 2
    eye = jnp.eye(half_g, dtype=F32)

    def in_blocks(bb):
        t = bb.reshape(S5_GROUP, 2, half_g, S5_STATE)
        return jnp.einsum('ckgp,gh->kgchp', t, eye).reshape(2, half_g * S5_GROUP, half_g * S5_STATE).astype(BF16)

    def out_blocks(cc):
        t = cc.reshape(2, half_g, S5_GROUP, S5_STATE)
        return jnp.einsum('kgop,gh->kgpho', t, eye).reshape(2, half_g * S5_STATE, half_g * S5_GROUP).astype(BF16)

    def lane_blocks(pw):
        return pw.reshape(SCAN_LEVELS, S5_LANE_BLOCKS, LANES).transpose(1, 0, 2)

    rep = lambda w: jnp.repeat(w, HEAD_DIM, axis=-1)
    args = (x, gain.reshape(1, D_MODEL),
            w_in[:, :su1].astype(BF16), in_blocks(bb_re), in_blocks(bb_im),
            lane_blocks(pow_re), lane_blocks(pow_im), out_blocks(c_re), out_blocks(c_im),
            d_skip.reshape(1, MIX_W), w_glu.astype(BF16), b_glu.reshape(1, MIX_W),
            w_in[:, su1:z1].astype(BF16), rep(w_in[:, z1:z1 + HEADS]).astype(BF16),
            rep(w_in[:, z1 + HEADS:]).astype(BF16), conv_w, rep(a_log).reshape(1, MIX_W),
            rep(dt_bias).reshape(1, MIX_W), _head_tile(gdn_norm, HEADS), _seg_matrix(),
            _block_tril(tile), _block_ones(tile), w_out[:MIX_W].astype(BF16), w_out[MIX_W:].astype(BF16))
    in_specs = [_tile_spec(tile, D_MODEL)] + [_const_spec(a.shape) for a in args[1:]]
    return pl.pallas_call(
        _odd_kernel,
        grid=(bsz, seq // tile),
        in_specs=in_specs,
        out_specs=_tile_spec(tile, D_MODEL),
        out_shape=jax.ShapeDtypeStruct(x.shape, F32),
        scratch_shapes=[pltpu.VMEM((3, 2, S5_LANE_BLOCKS, SUBLANES + tile, LANES), F32),
                        pltpu.VMEM((2, S5_LANE_BLOCKS, SUBLANES, LANES), F32),
                        pltpu.VMEM((2, S5_LANE_BLOCKS, tile, LANES), BF16),
                        pltpu.VMEM((PAIRS, PAIR_W, PAIR_W), F32),
                        pltpu.VMEM((3 * MIX_W // LANES, SUBLANES + tile, LANES), F32),
                        pltpu.VMEM((tile, MIX_W), F32),
                        pltpu.VMEM((tile, MIX_W), BF16),
                        pltpu.VMEM((tile // CHUNK, PAIRS, CHUNK, 2 * PAIR_W), BF16),
                        pltpu.VMEM((tile // CHUNK, PAIRS, PAIR_W, PAIR_W), F32),
                        pltpu.VMEM((tile // CHUNK, PAIRS, PAIR_W + CHUNK, PAIR_W), BF16)],
        compiler_params=_params(),
        name="s5_gdn_mixer",
    )(*args)


def kernel(x, norm_mix, norm_ffn, w_in_even, w_gla_gate, b_gla_gate, gla_out_norm, swa_q_norm, swa_k_norm, swa_sinks, w_out_even, w_in_odd, s5_a_re, s5_a_im, s5_log_dt, s5_b_re, s5_b_im, s5_c_re, s5_c_im, s5_d, s5_w_glu, s5_b_glu, gdn_conv_w, gdn_a_log, gdn_dt_bias, gdn_out_norm, w_out_odd, w_ffn_up, ffn_conv_w, ffn_conv_b, w_ffn_down):
    depth = norm_mix.shape[0]
    ffn_gains = norm_ffn.reshape(depth, 1, D_MODEL)
    ffn_up = w_ffn_up.astype(BF16)
    ffn_down = w_ffn_down.astype(BF16)
    ffn_bias = ffn_conv_b.reshape(depth, 1, 2 * FFN_DIM)
    for layer in range(depth):
        i = layer // 2
        if layer % 2 == 0:
            x = _even_layer(x, norm_mix[layer], w_in_even[i], w_gla_gate[i], b_gla_gate[i],
                            gla_out_norm[i], swa_q_norm[i], swa_k_norm[i], swa_sinks[i], w_out_even[i])
        else:
            x = _odd_layer(x, norm_mix[layer], w_in_odd[i], s5_a_re[i], s5_a_im[i], s5_log_dt[i],
                           s5_b_re[i], s5_b_im[i], s5_c_re[i], s5_c_im[i], s5_d[i], s5_w_glu[i],
                           s5_b_glu[i], gdn_conv_w[i], gdn_a_log[i], gdn_dt_bias[i], gdn_out_norm[i],
                           w_out_odd[i])
        x = _ffn_layer(x, layer, ffn_gains, ffn_up, ffn_conv_w, ffn_bias, ffn_down)
    return x
```

```python
import math

import jax
import jax.numpy as jnp
from jax import lax
from jax.experimental import pallas as pl
from jax.experimental.pallas import tpu as pltpu

F32 = jnp.float32
BF16 = jnp.bfloat16

D_MODEL = 1024
HEAD_DIM = 64
HEADS = 8
MIX_W = HEADS * HEAD_DIM
PAIRS = HEADS // 2
PAIR_W = 2 * HEAD_DIM
GLA_RANK = 16
GLA_GATE_NORM = 16.0
CHUNK = 64
SWA_BLOCK = 128
SWA_KV_HEADS = 2
S5_GROUP = 16
S5_GROUPS = 32
S5_STATE = 64
S5_LANES = S5_GROUPS * S5_STATE
GDN_CONV = 4
FFN_DIM = 2816
FFN_CONV = 3
NORM_EPS = 1e-6
LANES = 128
SUBLANES = 8
MXU_COLS = 256
VMEM_LIMIT = 56 * 1024 * 1024
S5_LANE_BLOCKS = S5_LANES // LANES

MIX_TILE = 256
FFN_TILE = 512
FFN_COLS = MXU_COLS
SCAN_LEVELS = int(math.log2(SUBLANES)) + 1
SCAN_UNROLL = 8
MXU = "mxu"
VPU = "vpu"
EVEN_WEIGHTS = (1, 2)
ODD_WEIGHTS = (1, 1)
GDN_GROUP = 4


def _dot(a, b):
    return jnp.dot(a, b, preferred_element_type=F32)


def _dot_nt(a, b):
    return lax.dot_general(a, b, (((1,), (1,)), ((), ())), preferred_element_type=F32)


def _dot_tn(a, b):
    return lax.dot_general(a, b, (((0,), (0,)), ((), ())), preferred_element_type=F32)


def _dot01_lhs(m01, x):
    hi = x.astype(BF16)
    lo = (x - hi.astype(F32)).astype(BF16)
    return _dot(m01, hi) + _dot(m01, lo)


def _sigmoid(x):
    return 1.0 / (1.0 + jnp.exp(-x))


def _silu(x):
    return x * _sigmoid(x)


def _softplus(x):
    return jnp.maximum(x, 0.0) + jnp.log1p(jnp.exp(-jnp.abs(x)))


def _rms_rows(x, gain_row):
    ms = jnp.mean(x * x, axis=-1, keepdims=True)
    return x * lax.rsqrt(ms + NORM_EPS) * gain_row


def _seg_mean(y, seg01):
    return _dot(y.astype(BF16), seg01) * (1.0 / HEAD_DIM)


def _block_diag_pair(v):
    first = lax.broadcasted_iota(jnp.int32, v.shape, 1) < HEAD_DIM
    zero = jnp.zeros_like(v)
    return jnp.concatenate([jnp.where(first, v, zero), jnp.where(first, zero, v)], axis=0)


def _same_head_mask():
    r = lax.broadcasted_iota(jnp.int32, (PAIR_W, PAIR_W), 0)
    c = lax.broadcasted_iota(jnp.int32, (PAIR_W, PAIR_W), 1)
    return (r < HEAD_DIM) == (c < HEAD_DIM)


def _causal_conv(u_ref, j, taps_ref, lanes, tile, current, acc=None):
    taps = taps_ref.shape[0]
    for k in range(taps):
        off = SUBLANES - (taps - 1) + k
        rows = current if off == SUBLANES else u_ref[j, off:off + tile, :]
        term = taps_ref[k:k + 1, lanes] * rows
        acc = term if acc is None else acc + term
    return acc


def _interleave(streams, weights):
    results = [None] * len(streams)
    live = list(range(len(streams)))
    while live:
        for i in list(live):
            for _ in range(weights[i]):
                try:
                    next(streams[i])
                except StopIteration as stop:
                    results[i] = stop.value
                    live.remove(i)
                    break
    return results


def _const_spec(shape):
    nd = len(shape)
    return pl.BlockSpec(shape, lambda b, l: (0,) * nd, pipeline_mode=pl.Buffered(1))


def _tile_spec(tile, width):
    return pl.BlockSpec((1, tile, width), lambda b, l: (b, l, 0))


def _params():
    return pltpu.CompilerParams(dimension_semantics=("arbitrary", "arbitrary"),
                                vmem_limit_bytes=VMEM_LIMIT)


def _ffn_kernel(x_ref, gain_ref, wa_ref, wg_ref, cwa_ref, cwg_ref, cba_ref, cbg_ref, wd_ref,
                o_ref, ua_ref, ug_ref, act_ref):
    tile = x_ref.shape[1]
    blocks = FFN_COLS // LANES

    @pl.when(pl.program_id(1) == 0)
    def _():
        ua_ref[:, 0:SUBLANES, :] = jnp.zeros((FFN_DIM // LANES, SUBLANES, LANES), F32)
        ug_ref[:, 0:SUBLANES, :] = jnp.zeros((FFN_DIM // LANES, SUBLANES, LANES), F32)

    x = x_ref[0]
    h = _rms_rows(x, gain_ref[...]).astype(BF16)
    for c in range(FFN_DIM // FFN_COLS):
        sl = slice(c * FFN_COLS, (c + 1) * FFN_COLS)
        ua = _dot(h, wa_ref[:, sl])
        ug = _dot(h, wg_ref[:, sl])
        for t in range(blocks):
            j = c * blocks + t
            lanes = slice(j * LANES, (j + 1) * LANES)
            ua_t = ua[:, t * LANES:(t + 1) * LANES]
            ug_t = ug[:, t * LANES:(t + 1) * LANES]
            ua_ref[j, SUBLANES:SUBLANES + tile, :] = ua_t
            ug_ref[j, SUBLANES:SUBLANES + tile, :] = ug_t
            a = _causal_conv(ua_ref, j, cwa_ref, lanes, tile, ua_t, cba_ref[:, lanes])
            g = _causal_conv(ug_ref, j, cwg_ref, lanes, tile, ug_t, cbg_ref[:, lanes])
            act_ref[:, lanes] = (_silu(g) * a).astype(BF16)
    ua_ref[:, 0:SUBLANES, :] = ua_ref[:, tile:tile + SUBLANES, :]
    ug_ref[:, 0:SUBLANES, :] = ug_ref[:, tile:tile + SUBLANES, :]
    o_ref[0] = x + _dot(act_ref[...], wd_ref[...])


def _layer_spec(rows, cols, layer, col_block=0):
    return pl.BlockSpec((None, rows, cols), lambda b, l: (layer, 0, col_block),
                        pipeline_mode=pl.Buffered(1))


def _ffn_layer(x, layer, gains, w_up, conv_w, conv_b, w_down):
    bsz, seq, _ = x.shape
    tile = min(FFN_TILE, seq)
    args = (x, gains, w_up, w_up, conv_w, conv_w, conv_b, conv_b, w_down)
    in_specs = [_tile_spec(tile, D_MODEL),
                _layer_spec(1, D_MODEL, layer),
                _layer_spec(D_MODEL, FFN_DIM, layer, 0), _layer_spec(D_MODEL, FFN_DIM, layer, 1),
                _layer_spec(FFN_CONV, FFN_DIM, layer, 0), _layer_spec(FFN_CONV, FFN_DIM, layer, 1),
                _layer_spec(1, FFN_DIM, layer, 0), _layer_spec(1, FFN_DIM, layer, 1),
                _layer_spec(FFN_DIM, D_MODEL, layer)]
    return pl.pallas_call(
        _ffn_kernel,
        grid=(bsz, seq // tile),
        in_specs=in_specs,
        out_specs=_tile_spec(tile, D_MODEL),
        out_shape=jax.ShapeDtypeStruct(x.shape, F32),
        scratch_shapes=[pltpu.VMEM((FFN_DIM // LANES, SUBLANES + tile, LANES), F32),
                        pltpu.VMEM((FFN_DIM // LANES, SUBLANES + tile, LANES), F32),
                        pltpu.VMEM((tile, FFN_DIM), BF16)],
        compiler_params=_params(),
        name="conv_ffn",
    )(*args)


def _gla_steps(h, tile, wmain_ref, wglr_ref, wgate_ref, bgate_ref, glan_ref, seg_ref, tril_ref,
               st_ref, oa_ref):
    yield MXU
    proj = _dot(h, wmain_ref[:, 0:4 * MIX_W])
    gq = proj[:, 0 * MIX_W:1 * MIX_W]
    gk = proj[:, 1 * MIX_W:2 * MIX_W]
    gv = proj[:, 2 * MIX_W:3 * MIX_W]
    gr = proj[:, 3 * MIX_W:4 * MIX_W]
    glr = _dot(h, wglr_ref[...])
    gate = _dot(glr.astype(BF16), wgate_ref[...]) + bgate_ref[...]
    yield VPU
    log_g = (jnp.minimum(gate, 0.0) - jnp.log1p(jnp.exp(-jnp.abs(gate)))) * (1.0 / GLA_GATE_NORM)
    yield MXU
    cum = _dot01_lhs(tril_ref[...], log_g)
    yield VPU
    q_pos = (gq * HEAD_DIM ** -0.5) * jnp.exp(cum)
    k_neg = gk * jnp.exp(-cum)
    row = lax.broadcasted_iota(jnp.int32, (CHUNK, PAIR_W), 0)
    lane = lax.broadcasted_iota(jnp.int32, (CHUNK, PAIR_W), 1)
    causal = row >= (lane % CHUNK)
    same_head = _same_head_mask()
    rows = [slice(c * CHUNK, (c + 1) * CHUNK) for c in range(tile // CHUNK)]
    lanes = [slice(p * PAIR_W, (p + 1) * PAIR_W) for p in range(PAIRS)]
    cum_last = [cum[rs][CHUNK - 1:CHUNK] for rs in rows]
    k_dec = jnp.concatenate([gk[rs] * jnp.exp(cl - cum[rs]) for rs, cl in zip(rows, cum_last)], axis=0)
    q_b = q_pos.astype(BF16)
    kn_b = k_neg.astype(BF16)
    kd_b = k_dec.astype(BF16)
    v_b = gv.astype(BF16)
    scores, o_intra, upd = {}, {}, {}
    for c, rs in enumerate(rows):
        yield MXU
        for p, ls in enumerate(lanes):
            s = _dot_nt(q_b[rs, ls], _block_diag_pair(kn_b[rs, ls]))
            scores[c, p] = jnp.where(causal, s, 0.0).astype(BF16)
    for c, rs in enumerate(rows):
        yield MXU
        for p, ls in enumerate(lanes):
            upd[c, p] = jnp.where(same_head, _dot_tn(v_b[rs, ls], kd_b[rs, ls]), 0.0)
    for c, rs in enumerate(rows):
        yield MXU
        for p, ls in enumerate(lanes):
            o_intra[c, p] = _dot(scores[c, p], _block_diag_pair(v_b[rs, ls]))
    yield VPU
    states = {}
    for p, ls in enumerate(lanes):
        state_t = st_ref[p]
        for c, cl in enumerate(cum_last):
            states[c, p] = state_t
            state_t = state_t * jnp.exp(cl[:, ls]) + upd[c, p]
        st_ref[p] = state_t
    for c, rs in enumerate(rows):
        yield MXU
        for p, ls in enumerate(lanes):
            oa_ref[rs, ls] = o_intra[c, p] + _dot_nt(q_b[rs, ls], states[c, p].astype(BF16))
    yield MXU
    o_a = oa_ref[...]
    ms = _seg_mean(o_a * o_a, seg_ref[...])
    yield VPU
    return (o_a * lax.rsqrt(ms + NORM_EPS) * glan_ref[...] * _silu(gr)).astype(BF16)


def _swa_steps(h, tile, seq_tile, sinks_ref, wmain_ref, wkv_ref, qn_ref, kn_ref, seg_ref, bias_ref,
               kvs_ref, ob_ref):
    yield MXU
    sq = _dot(h, wmain_ref[:, 4 * MIX_W:5 * MIX_W])
    kv = _dot(h, wkv_ref[...])
    seg = seg_ref[...]
    yield MXU
    ms_q = _seg_mean(sq * sq, seg)
    sk = kv[:, 0:PAIR_W]
    ms_k = _seg_mean(sk * sk, seg[0:PAIR_W, 0:PAIR_W])
    yield VPU
    qn = (sq * lax.rsqrt(ms_q + NORM_EPS) * qn_ref[...] * HEAD_DIM ** -0.5).astype(BF16)
    kvs_ref[SWA_BLOCK:SWA_BLOCK + tile, 0:PAIR_W] = sk * lax.rsqrt(ms_k + NORM_EPS) * kn_ref[...]
    kvs_ref[SWA_BLOCK:SWA_BLOCK + tile, PAIR_W:2 * PAIR_W] = kv[:, PAIR_W:2 * PAIR_W]
    skey = lax.broadcasted_iota(jnp.int32, (2 * SWA_BLOCK, 2 * SWA_BLOCK), 1)
    top_col = lax.broadcasted_iota(jnp.int32, (2 * SWA_BLOCK, 1), 0) < SWA_BLOCK
    ones_cols = jnp.ones((2 * SWA_BLOCK, PAIR_W), BF16)
    first_lane = lax.broadcasted_iota(jnp.int32, (2 * SWA_BLOCK, PAIR_W), 1) < HEAD_DIM
    no_prev = jnp.where(skey < SWA_BLOCK, jnp.where(seq_tile > 0, 0.0, -jnp.inf), 0.0)
    for j in range(tile // SWA_BLOCK):
        ks = slice(j * SWA_BLOCK, (j + 2) * SWA_BLOCK)
        keys = kvs_ref[ks, 0:PAIR_W].astype(BF16)
        vals = kvs_ref[ks, PAIR_W:2 * PAIR_W].astype(BF16)
        zero_v = jnp.zeros_like(vals)
        vals0 = jnp.concatenate([jnp.where(first_lane, vals, zero_v), ones_cols], axis=1)
        vals1 = jnp.concatenate([jnp.where(first_lane, zero_v, vals), ones_cols], axis=1)
        for p in range(PAIRS):
            yield MXU
            qs = _block_diag_pair(qn[j * SWA_BLOCK:(j + 1) * SWA_BLOCK, p * PAIR_W:(p + 1) * PAIR_W])
            s = _dot_nt(qs, keys)
            yield VPU
            s = s + bias_ref[p]
            if j == 0:
                s = s + no_prev
            sink = jnp.where(top_col, sinks_ref[p], sinks_ref[p + PAIRS])
            m = jnp.maximum(jnp.max(s, axis=-1, keepdims=True), sink)
            pe = jnp.exp(s - m).astype(BF16)
            sink_e = jnp.exp(sink - m)
            yield MXU
            r0 = _dot(pe[0:SWA_BLOCK], vals0)
            r1 = _dot(pe[SWA_BLOCK:], vals1)
            o_pair = (r0[:, 0:PAIR_W] / (r0[:, PAIR_W:] + sink_e[0:SWA_BLOCK])
                      + r1[:, 0:PAIR_W] / (r1[:, PAIR_W:] + sink_e[SWA_BLOCK:]))
            ob_ref[j * SWA_BLOCK:(j + 1) * SWA_BLOCK, p * PAIR_W:(p + 1) * PAIR_W] = o_pair.astype(BF16)
    kvs_ref[0:SWA_BLOCK, :] = kvs_ref[tile:tile + SWA_BLOCK, :]
    return ob_ref[...]


def _even_kernel(sinks_ref, x_ref, gain_ref, wmain_ref, wkv_ref, wglr_ref, wgate_ref, bgate_ref,
                 glan_ref, qn_ref, kn_ref, seg_ref, tril_ref, bias_ref, woa_ref, wob_ref,
                 o_ref, st_ref, kvs_ref, oa_ref, ob_ref):
    tile = x_ref.shape[1]
    seq_tile = pl.program_id(1)

    @pl.when(seq_tile == 0)
    def _():
        st_ref[...] = jnp.zeros_like(st_ref)
        kvs_ref[0:SWA_BLOCK, :] = jnp.zeros((SWA_BLOCK, 2 * PAIR_W), F32)

    x = x_ref[0]
    h = _rms_rows(x, gain_ref[...]).astype(BF16)
    o_a, o_b = _interleave(
        [_gla_steps(h, tile, wmain_ref, wglr_ref, wgate_ref, bgate_ref, glan_ref, seg_ref, tril_ref,
                    st_ref, oa_ref),
         _swa_steps(h, tile, seq_tile, sinks_ref, wmain_ref, wkv_ref, qn_ref, kn_ref, seg_ref, bias_ref,
                    kvs_ref, ob_ref)],
        EVEN_WEIGHTS)
    o_ref[0] = x + _dot(o_a, woa_ref[...]) + _dot(o_b, wob_ref[...])


def _block_tril(tile):
    r = jnp.arange(tile)
    same = (r[:, None] // CHUNK) == (r[None, :] // CHUNK)
    return (same & (r[:, None] >= r[None, :])).astype(BF16)


def _block_ones(tile):
    r = jnp.arange(tile)
    return ((r[:, None] // CHUNK) == (r[None, :] // CHUNK)).astype(BF16)


def _seg_matrix():
    r = jnp.arange(MIX_W)
    return ((r[:, None] // HEAD_DIM) == (r[None, :] // HEAD_DIM)).astype(BF16)


def _swa_bias():
    r = jnp.arange(2 * SWA_BLOCK)
    dist = (r[:, None] % SWA_BLOCK) + SWA_BLOCK - r[None, :]
    valid = (dist >= 0) & (dist < SWA_BLOCK)
    head = jnp.arange(PAIRS)[:, None, None] + jnp.where(r < SWA_BLOCK, 0, PAIRS)[None, :, None]
    slope = jnp.exp2(-(head + 1).astype(F32))
    return jnp.where(valid[None], -slope * dist[None].astype(F32), -jnp.inf)


def _head_tile(v, reps):
    return jnp.tile(v.astype(F32), reps).reshape(1, reps * v.shape[0])


def _even_layer(x, gain, w_in, w_gate, b_gate, gla_norm, q_norm, k_norm, sinks, w_out):
    bsz, seq, _ = x.shape
    tile = min(2 * MIX_TILE, seq)
    order = jnp.array([h for p in range(PAIRS) for h in (p, p + PAIRS)])
    c0 = 4 * MIX_W
    glr0 = c0
    sq0 = c0 + GLA_RANK
    sk0 = sq0 + MIX_W
    w_sq = w_in[:, sq0:sk0].reshape(D_MODEL, HEADS, HEAD_DIM)[:, order].reshape(D_MODEL, MIX_W)
    w_main = jnp.concatenate([w_in[:, :c0], w_sq], axis=1).astype(BF16)
    w_kv = w_in[:, sk0:].astype(BF16)
    w_glr = jnp.pad(w_in[:, glr0:sq0], ((0, 0), (0, LANES - GLA_RANK))).astype(BF16)
    w_gate_p = jnp.pad(w_gate, ((0, LANES - GLA_RANK), (0, 0))).astype(BF16)
    w_out_a = w_out[:MIX_W].astype(BF16)
    w_out_b = w_out[MIX_W:].reshape(HEADS, HEAD_DIM, D_MODEL)[order].reshape(MIX_W, D_MODEL).astype(BF16)
    args = (x, gain.reshape(1, D_MODEL), w_main, w_kv, w_glr, w_gate_p, b_gate.reshape(1, MIX_W),
            _head_tile(gla_norm, HEADS), _head_tile(q_norm, HEADS), _head_tile(k_norm, SWA_KV_HEADS),
            _seg_matrix(), _block_tril(tile), _swa_bias(), w_out_a, w_out_b)
    in_specs = ([pl.BlockSpec(memory_space=pltpu.SMEM), _tile_spec(tile, D_MODEL)]
                + [_const_spec(a.shape) for a in args[1:]])
    return pl.pallas_call(
        _even_kernel,
        grid=(bsz, seq // tile),
        in_specs=in_specs,
        out_specs=_tile_spec(tile, D_MODEL),
        out_shape=jax.ShapeDtypeStruct(x.shape, F32),
        scratch_shapes=[pltpu.VMEM((PAIRS, PAIR_W, PAIR_W), F32),
                        pltpu.VMEM((SWA_BLOCK + tile, 2 * PAIR_W), F32),
                        pltpu.VMEM((tile, MIX_W), F32),
                        pltpu.VMEM((tile, MIX_W), BF16)],
        compiler_params=_params(),
        name="gla_swa_mixer",
    )(sinks.astype(F32), *args)


def _s5_prep_kernel(are_ref, aim_ref, ldt_ref, bre_ref, bim_ref, pre_ref, pim_ref, bbre_ref, bbim_ref):
    a_re = are_ref[...]
    a_im = aim_ref[...]
    dt = jnp.exp(ldt_ref[...])
    for k in range(SCAN_LEVELS):
        step = float(2 ** k)
        mag = jnp.exp(step * dt * a_re)
        ang = step * dt * a_im
        pre_ref[k] = mag * jnp.cos(ang)
        pim_ref[k] = mag * jnp.sin(ang)
    abar_re = pre_ref[0]
    abar_im = pim_ref[0]
    den = a_re * a_re + a_im * a_im
    f_re = ((abar_re - 1.0) * a_re + abar_im * a_im) / den
    f_im = (abar_im * a_re - (abar_re - 1.0) * a_im) / den
    for c in range(S5_GROUP):
        bbre_ref[c] = f_re * bre_ref[c] - f_im * bim_ref[c]
        bbim_ref[c] = f_re * bim_ref[c] + f_im * bre_ref[c]


def _s5_steps(h, tile, wsu_ref, bre_ref, bim_ref, pre_ref, pim_ref, cre_ref, cim_ref,
              dskip_ref, wglu_ref, bglu_ref, buf_ref, st_ref, xs_ref):
    half_in = MIX_W // 2
    half_blocks = S5_LANE_BLOCKS // 2
    body = slice(SUBLANES, SUBLANES + tile)
    tail = slice(tile, tile + SUBLANES)
    head = slice(0, SUBLANES)
    yield MXU
    u = _dot(h, wsu_ref[...])
    ub = u.astype(BF16)
    for kt in range(2):
        yield MXU
        u_half = ub[:, kt * half_in:(kt + 1) * half_in]
        bu_re = _dot(u_half, bre_ref[kt])
        bu_im = _dot(u_half, bim_ref[kt])
        for j in range(half_blocks):
            lb = kt * half_blocks + j
            buf_ref[0, 0, lb, body, :] = bu_re[:, j * LANES:(j + 1) * LANES]
            buf_ref[0, 1, lb, body, :] = bu_im[:, j * LANES:(j + 1) * LANES]

    def scan_blocks(blocks):
        for lb in blocks:
            yield VPU
            p_re = pre_ref[lb]
            p_im = pim_ref[lb]
            for k, (src, dst) in enumerate(((0, 1), (1, 2), (2, 1))):
                shift = 2 ** k
                shifted = slice(SUBLANES - shift, SUBLANES - shift + tile)
                cur_re = buf_ref[src, 0, lb, body, :]
                cur_im = buf_ref[src, 1, lb, body, :]
                sh_re = buf_ref[src, 0, lb, shifted, :]
                sh_im = buf_ref[src, 1, lb, shifted, :]
                a_re = p_re[k:k + 1]
                a_im = p_im[k:k + 1]
                new_re = cur_re + a_re * sh_re - a_im * sh_im
                new_im = cur_im + a_re * sh_im + a_im * sh_re
                buf_ref[src, 0, lb, head, :] = buf_ref[src, 0, lb, tail, :]
                buf_ref[src, 1, lb, head, :] = buf_ref[src, 1, lb, tail, :]
                buf_ref[dst, 0, lb, body, :] = new_re
                buf_ref[dst, 1, lb, body, :] = new_im
        a8 = [(pre_ref[lb][3:4], pim_ref[lb][3:4]) for lb in blocks]
        state = [(st_ref[0, lb], st_ref[1, lb]) for lb in blocks]
        groups = tile // SUBLANES
        for g in range(groups):
            if g % (groups // 4) == 0:
                yield VPU
            grp = slice(SUBLANES * (g + 1), SUBLANES * (g + 2))
            for n, lb in enumerate(blocks):
                a_re, a_im = a8[n]
                s_re, s_im = state[n]
                x_re = buf_ref[1, 0, lb, grp, :] + a_re * s_re - a_im * s_im
                x_im = buf_ref[1, 1, lb, grp, :] + a_re * s_im + a_im * s_re
                buf_ref[1, 0, lb, grp, :] = x_re
                buf_ref[1, 1, lb, grp, :] = x_im
                state[n] = (x_re, x_im)
        yield VPU
        for n, lb in enumerate(blocks):
            st_ref[0, lb] = state[n][0]
            st_ref[1, lb] = state[n][1]
            xs_ref[0, lb] = buf_ref[1, 0, lb, body, :].astype(BF16)
            xs_ref[1, lb] = buf_ref[1, 1, lb, body, :].astype(BF16)

    ys = []
    for kt in range(2):
        blocks = list(range(kt * half_blocks, (kt + 1) * half_blocks))
        for n in range(0, half_blocks, SCAN_UNROLL):
            yield from scan_blocks(blocks[n:n + SCAN_UNROLL])
        yield MXU
        x_re = jnp.concatenate([xs_ref[0, lb] for lb in blocks], axis=1)
        x_im = jnp.concatenate([xs_ref[1, lb] for lb in blocks], axis=1)
        ys.append(_dot(x_re, cre_ref[kt]) - _dot(x_im, cim_ref[kt]))
    yield VPU
    y = jnp.concatenate(ys, axis=1) + dskip_ref[...] * u
    y = 0.5 * y * (1.0 + jnp.tanh(math.sqrt(2.0 / math.pi) * (y + 0.044715 * (y * y * y))))
    yield MXU
    gate = _dot(y.astype(BF16), wglu_ref[...]) + bglu_ref[...]
    yield VPU
    return (y * _sigmoid(gate)).astype(BF16)


def _gdn_steps(h, tile, wqkvz_ref, wda_ref, wdb_ref, cw_ref, alog_ref, dtb_ref, gnorm_ref, seg_ref,
               tril_ref, onesbd_ref, st_ref, cs_ref, od_ref, attn_ref, uw_ref, ku_ref, kwq_ref):
    qkv_w = 3 * MIX_W
    yield MXU
    da = _dot(h, wda_ref[...])
    db = _dot(h, wdb_ref[...])
    yield VPU
    log_alpha = -jnp.exp(alog_ref[...]) * _softplus(da + dtb_ref[...])
    beta = _sigmoid(db)
    row = lax.broadcasted_iota(jnp.int32, (tile, MIX_W), 0) % CHUNK
    key = lax.broadcasted_iota(jnp.int32, (tile, MIX_W), 1) % CHUNK
    yield MXU
    g = _dot01_lhs(tril_ref[...], log_alpha)
    yield MXU
    g_key = _dot01_lhs(onesbd_ref[...], jnp.where(row <= key, log_alpha, 0.0))
    yield MXU
    proj = _dot(h, wqkvz_ref[:, 0:qkv_w])
    yield VPU
    decay_incl = jnp.exp(jnp.where(row >= key, g - g_key, -jnp.inf))
    decay_strict = jnp.where(row > key, decay_incl, 0.0)
    exp_g = jnp.exp(g)
    conv = []
    for j in range(qkv_w // LANES):
        if j % 4 == 0:
            yield VPU
        lanes_j = slice(j * LANES, (j + 1) * LANES)
        cs_ref[j, SUBLANES:SUBLANES + tile, :] = proj[:, lanes_j]
        conv.append(_silu(_causal_conv(cs_ref, j, cw_ref, lanes_j, tile, proj[:, lanes_j])))
    cs_ref[:, 0:SUBLANES, :] = cs_ref[:, tile:tile + SUBLANES, :]
    qkv = jnp.concatenate(conv, axis=1)
    q = qkv[:, 0:MIX_W]
    k_ = qkv[:, MIX_W:2 * MIX_W]
    v = qkv[:, 2 * MIX_W:]
    yield MXU
    seg = seg_ref[...]
    ss_q = _seg_mean(q * q, seg) * HEAD_DIM
    ss_k = _seg_mean(k_ * k_, seg) * HEAD_DIM
    z = _dot(h, wqkvz_ref[:, qkv_w:])
    yield VPU
    q = q * lax.rsqrt(ss_q + NORM_EPS) * HEAD_DIM ** -0.5
    k_ = k_ * lax.rsqrt(ss_k + NORM_EPS)
    rows = [slice(c * CHUNK, (c + 1) * CHUNK) for c in range(tile // CHUNK)]
    lanes = [slice(p * PAIR_W, (p + 1) * PAIR_W) for p in range(PAIRS)]
    g_last = [g[rs][CHUNK - 1:CHUNK] for rs in rows]
    k_dec = jnp.concatenate([k_[rs] * jnp.exp(gl - g[rs]) for rs, gl in zip(rows, g_last)], axis=0)
    k_b = k_.astype(BF16)
    q_b = q.astype(BF16)
    kd_b = k_dec.astype(BF16)
    qd_b = (q * exp_g).astype(BF16)
    vb_b = (v * beta).astype(BF16)
    kbg_b = (k_ * beta * exp_g).astype(BF16)
    eye_pair = (lax.broadcasted_iota(jnp.int32, (CHUNK, PAIR_W), 0)
                == lax.broadcasted_iota(jnp.int32, (CHUNK, PAIR_W), 1) % CHUNK).astype(F32)
    same_head = _same_head_mask()
    first_lane = lax.broadcasted_iota(jnp.int32, (CHUNK, PAIR_W), 1) < HEAD_DIM
    zero = jnp.zeros((CHUNK, PAIR_W), BF16)
    power, power_bd, inv = {}, {}, {}
    for first in range(0, len(rows), GDN_GROUP):
        group = range(first, first + GDN_GROUP)
        for c in group:
            yield MXU
            rs = rows[c]
            for p, ls in enumerate(lanes):
                sc = _dot_nt(jnp.concatenate([k_b[rs, ls], q_b[rs, ls]], axis=0), _block_diag_pair(k_b[rs, ls]))
                lower = beta[rs, ls] * sc[0:CHUNK] * decay_strict[rs, ls]
                power[c, p] = lower.astype(BF16)
                power_bd[c, p] = _block_diag_pair(power[c, p])
                inv[c, p] = eye_pair - lower
                attn_ref[rs, ls] = (sc[CHUNK:] * decay_incl[rs, ls]).astype(BF16)
        for _ in range(int(math.log2(CHUNK)) - 1):
            for c in group:
                yield MXU
                for p in range(PAIRS):
                    power[c, p] = _dot(power[c, p], power_bd[c, p]).astype(BF16)
                    power_bd[c, p] = _block_diag_pair(power[c, p])
            for c in group:
                yield MXU
                for p in range(PAIRS):
                    inv[c, p] = inv[c, p] + _dot(inv[c, p].astype(BF16), power_bd[c, p])
        for c in group:
            yield MXU
            rs = rows[c]
            for p, ls in enumerate(lanes):
                vb = vb_b[rs, ls]
                kbg = kbg_b[rs, ls]
                rhs = jnp.concatenate(
                    [jnp.concatenate([jnp.where(first_lane, vb, zero), jnp.where(first_lane, kbg, zero)], axis=1),
                     jnp.concatenate([jnp.where(first_lane, zero, vb), jnp.where(first_lane, zero, kbg)], axis=1)],
                    axis=0)
                uw_ref[c, p] = _dot(inv[c, p].astype(BF16), rhs).astype(BF16)
    for c, rs in enumerate(rows):
        yield MXU
        for p, ls in enumerate(lanes):
            uw = uw_ref[c, p]
            w_b = uw[:, PAIR_W:]
            kw_ku = _dot_tn(kd_b[rs, ls], jnp.concatenate([w_b, uw[:, 0:PAIR_W]], axis=1))
            aw_au = _dot(attn_ref[rs, ls], jnp.concatenate(
                [_block_diag_pair(w_b), _block_diag_pair(uw[:, 0:PAIR_W])], axis=1))
            kw = jnp.where(same_head, kw_ku[:, 0:PAIR_W], 0.0).astype(BF16)
            ku_ref[c, p] = jnp.where(same_head, kw_ku[:, PAIR_W:], 0.0)
            q_eff = (qd_b[rs, ls].astype(F32) - aw_au[:, 0:PAIR_W]).astype(BF16)
            kwq_ref[c, p] = jnp.concatenate([kw, q_eff], axis=0)
            od_ref[rs, ls] = aw_au[:, PAIR_W:]
    for c, (rs, gl) in enumerate(zip(rows, g_last)):
        yield MXU
        state = [st_ref[p] for p in range(PAIRS)]
        prod = [_dot(kwq_ref[c, p], st.astype(BF16)) for p, st in enumerate(state)]
        for p, ls in enumerate(lanes):
            st_ref[p] = state[p] * jnp.exp(gl[:, ls]) - prod[p][0:PAIR_W] + ku_ref[c, p]
            od_ref[rs, ls] = od_ref[rs, ls] + prod[p][PAIR_W:]
    yield MXU
    o_d = od_ref[...]
    ms = _seg_mean(o_d * o_d, seg)
    yield VPU
    return (o_d * lax.rsqrt(ms + NORM_EPS) * gnorm_ref[...] * _silu(z)).astype(BF16)


def _odd_kernel(x_ref, gain_ref,
                wsu_ref, bre_ref, bim_ref, pre_ref, pim_ref, cre_ref, cim_ref, dskip_ref, wglu_ref, bglu_ref,
                wqkvz_ref, wda_ref, wdb_ref, cw_ref, alog_ref, dtb_ref, gnorm_ref, seg_ref, tril_ref,
                onesbd_ref, woc_ref, wod_ref,
                o_ref,
                buf_ref, sst_ref, xs_ref, gst_ref, cs_ref, od_ref, attn_ref, uw_ref, ku_ref, kwq_ref):
    tile = x_ref.shape[1]

    @pl.when(pl.program_id(1) == 0)
    def _():
        sst_ref[...] = jnp.zeros_like(sst_ref)
        buf_ref[:, :, :, 0:SUBLANES, :] = jnp.zeros((3, 2, S5_LANE_BLOCKS, SUBLANES, LANES), F32)
        gst_ref[...] = jnp.zeros_like(gst_ref)
        cs_ref[:, 0:SUBLANES, :] = jnp.zeros((3 * MIX_W // LANES, SUBLANES, LANES), F32)

    x = x_ref[0]
    h = _rms_rows(x, gain_ref[...]).astype(BF16)
    o_c, o_d = _interleave(
        [_s5_steps(h, tile, wsu_ref, bre_ref, bim_ref, pre_ref, pim_ref, cre_ref, cim_ref,
                   dskip_ref, wglu_ref, bglu_ref, buf_ref, sst_ref, xs_ref),
         _gdn_steps(h, tile, wqkvz_ref, wda_ref, wdb_ref, cw_ref, alog_ref, dtb_ref, gnorm_ref, seg_ref,
                    tril_ref, onesbd_ref, gst_ref, cs_ref, od_ref, attn_ref, uw_ref, ku_ref, kwq_ref)],
        ODD_WEIGHTS)
    o_ref[0] = x + _dot(o_c, woc_ref[...]) + _dot(o_d, wod_ref[...])


def _odd_layer(x, gain, w_in, a_re, a_im, log_dt, b_re, b_im, c_re, c_im, d_skip, w_glu, b_glu,
               conv_w, a_log, dt_bias, gdn_norm, w_out):
    bsz, seq, _ = x.shape
    tile = min(MIX_TILE, seq)
    su1 = MIX_W
    z1 = su1 + 4 * MIX_W
    gs = (S5_GROUPS, S5_STATE)
    pow_re, pow_im, bb_re, bb_im = pl.pallas_call(
        _s5_prep_kernel,
        out_shape=[jax.ShapeDtypeStruct((SCAN_LEVELS,) + gs, F32)] * 2
        + [jax.ShapeDtypeStruct((S5_GROUP,) + gs, F32)] * 2,
        name="s5_discretize",
    )(a_re, a_im, log_dt.reshape(S5_GROUPS, 1), b_re.transpose(2, 0, 1), b_im.transpose(2, 0, 1))
    half_g = S5_GROUPS // 2
    eye = jnp.eye(half_g, dtype=F32)

    def in_blocks(bb):
        t = bb.reshape(S5_GROUP, 2, half_g, S5_STATE)
        return jnp.einsum('ckgp,gh->kgchp', t, eye).reshape(2, half_g * S5_GROUP, half_g * S5_STATE).astype(BF16)

    def out_blocks(cc):
        t = cc.reshape(2, half_g, S5_GROUP, S5_STATE)
        return jnp.einsum('kgop,gh->kgpho', t, eye).reshape(2, half_g * S5_STATE, half_g * S5_GROUP).astype(BF16)

    def lane_blocks(pw):
        return pw.reshape(SCAN_LEVELS, S5_LANE_BLOCKS, LANES).transpose(1, 0, 2)

    rep = lambda w: jnp.repeat(w, HEAD_DIM, axis=-1)
    args = (x, gain.reshape(1, D_MODEL),
            w_in[:, :su1].astype(BF16), in_blocks(bb_re), in_blocks(bb_im),
            lane_blocks(pow_re), lane_blocks(pow_im), out_blocks(c_re), out_blocks(c_im),
            d_skip.reshape(1, MIX_W), w_glu.astype(BF16), b_glu.reshape(1, MIX_W),
            w_in[:, su1:z1].astype(BF16), rep(w_in[:, z1:z1 + HEADS]).astype(BF16),
            rep(w_in[:, z1 + HEADS:]).astype(BF16), conv_w, rep(a_log).reshape(1, MIX_W),
            rep(dt_bias).reshape(1, MIX_W), _head_tile(gdn_norm, HEADS), _seg_matrix(),
            _block_tril(tile), _block_ones(tile), w_out[:MIX_W].astype(BF16), w_out[MIX_W:].astype(BF16))
    in_specs = [_tile_spec(tile, D_MODEL)] + [_const_spec(a.shape) for a in args[1:]]
    return pl.pallas_call(
        _odd_kernel,
        grid=(bsz, seq // tile),
        in_specs=in_specs,
        out_specs=_tile_spec(tile, D_MODEL),
        out_shape=jax.ShapeDtypeStruct(x.shape, F32),
        scratch_shapes=[pltpu.VMEM((3, 2, S5_LANE_BLOCKS, SUBLANES + tile, LANES), F32),
                        pltpu.VMEM((2, S5_LANE_BLOCKS, SUBLANES, LANES), F32),
                        pltpu.VMEM((2, S5_LANE_BLOCKS, tile, LANES), BF16),
                        pltpu.VMEM((PAIRS, PAIR_W, PAIR_W), F32),
                        pltpu.VMEM((3 * MIX_W // LANES, SUBLANES + tile, LANES), F32),
                        pltpu.VMEM((tile, MIX_W), F32),
                        pltpu.VMEM((tile, MIX_W), BF16),
                        pltpu.VMEM((tile // CHUNK, PAIRS, CHUNK, 2 * PAIR_W), BF16),
                        pltpu.VMEM((tile // CHUNK, PAIRS, PAIR_W, PAIR_W), F32),
                        pltpu.VMEM((tile // CHUNK, PAIRS, PAIR_W + CHUNK, PAIR_W), BF16)],
        compiler_params=_params(),
        name="s5_gdn_mixer",
    )(*args)


def kernel(x, norm_mix, norm_ffn, w_in_even, w_gla_gate, b_gla_gate, gla_out_norm, swa_q_norm, swa_k_norm, swa_sinks, w_out_even, w_in_odd, s5_a_re, s5_a_im, s5_log_dt, s5_b_re, s5_b_im, s5_c_re, s5_c_im, s5_d, s5_w_glu, s5_b_glu, gdn_conv_w, gdn_a_log, gdn_dt_bias, gdn_out_norm, w_out_odd, w_ffn_up, ffn_conv_w, ffn_conv_b, w_ffn_down):
    depth = norm_mix.shape[0]
    ffn_gains = norm_ffn.reshape(depth, 1, D_MODEL)
    ffn_up = w_ffn_up.astype(BF16)
    ffn_down = w_ffn_down.astype(BF16)
    ffn_bias = ffn_conv_b.reshape(depth, 1, 2 * FFN_DIM)
    for layer in range(depth):
        i = layer // 2
        if layer % 2 == 0:
            x = _even_layer(x, norm_mix[layer], w_in_even[i], w_gla_gate[i], b_gla_gate[i],
                            gla_out_norm[i], swa_q_norm[i], swa_k_norm[i], swa_sinks[i], w_out_even[i])
        else:
            x = _odd_layer(x, norm_mix[layer], w_in_odd[i], s5_a_re[i], s5_a_im[i], s5_log_dt[i],
                           s5_b_re[i], s5_b_im[i], s5_c_re[i], s5_c_im[i], s5_d[i], s5_w_glu[i],
                           s5_b_glu[i], gdn_conv_w[i], gdn_a_log[i], gdn_dt_bias[i], gdn_out_norm[i],
                           w_out_odd[i])
        x = _ffn_layer(x, layer, ffn_gains, ffn_up, ffn_conv_w, ffn_bias, ffn_down)
    return x
```

```python
import math

import jax
import jax.numpy as jnp
from jax import lax
from jax.experimental import pallas as pl
from jax.experimental.pallas import tpu as pltpu

F32 = jnp.float32
BF16 = jnp.bfloat16

D_MODEL = 1024
HEAD_DIM = 64
HEADS = 8
MIX_W = HEADS * HEAD_DIM
PAIRS = HEADS // 2
PAIR_W = 2 * HEAD_DIM
GLA_RANK = 16
GLA_GATE_NORM = 16.0
CHUNK = 64
SWA_BLOCK = 128
SWA_KV_HEADS = 2
S5_GROUP = 16
S5_GROUPS = 32
S5_STATE = 64
S5_LANES = S5_GROUPS * S5_STATE
GDN_CONV = 4
FFN_DIM = 2816
FFN_CONV = 3
NORM_EPS = 1e-6
LANES = 128
SUBLANES = 8
MXU_COLS = 256
VMEM_LIMIT = 56 * 1024 * 1024
S5_LANE_BLOCKS = S5_LANES // LANES

MIX_TILE = 256
FFN_TILE = 256
FFN_COLS = MXU_COLS
SCAN_LEVELS = int(math.log2(SUBLANES)) + 1
SCAN_UNROLL = 8
MXU = "mxu"
VPU = "vpu"
EVEN_WEIGHTS = (1, 2)
ODD_WEIGHTS = (1, 1)
GDN_GROUP = 4


def _dot(a, b):
    return jnp.dot(a, b, preferred_element_type=F32)


def _dot_nt(a, b):
    return lax.dot_general(a, b, (((1,), (1,)), ((), ())), preferred_element_type=F32)


def _dot_tn(a, b):
    return lax.dot_general(a, b, (((0,), (0,)), ((), ())), preferred_element_type=F32)


def _dot01_lhs(m01, x):
    hi = x.astype(BF16)
    lo = (x - hi.astype(F32)).astype(BF16)
    return _dot(m01, hi) + _dot(m01, lo)


def _sigmoid(x):
    return 1.0 / (1.0 + jnp.exp(-x))


def _silu(x):
    return x * _sigmoid(x)


def _softplus(x):
    return jnp.maximum(x, 0.0) + jnp.log1p(jnp.exp(-jnp.abs(x)))


def _rms_rows(x, gain_row):
    ms = jnp.mean(x * x, axis=-1, keepdims=True)
    return x * lax.rsqrt(ms + NORM_EPS) * gain_row


def _seg_mean(y, seg01):
    return _dot(y.astype(BF16), seg01) * (1.0 / HEAD_DIM)


def _block_diag_pair(v):
    first = lax.broadcasted_iota(jnp.int32, v.shape, 1) < HEAD_DIM
    zero = jnp.zeros_like(v)
    return jnp.concatenate([jnp.where(first, v, zero), jnp.where(first, zero, v)], axis=0)


def _same_head_mask():
    r = lax.broadcasted_iota(jnp.int32, (PAIR_W, PAIR_W), 0)
    c = lax.broadcasted_iota(jnp.int32, (PAIR_W, PAIR_W), 1)
    return (r < HEAD_DIM) == (c < HEAD_DIM)


def _causal_conv(u_ref, j, taps_ref, lanes, tile, acc=None):
    taps = taps_ref.shape[0]
    for k in range(taps):
        off = SUBLANES - (taps - 1) + k
        term = taps_ref[k:k + 1, lanes] * u_ref[j, off:off + tile, :]
        acc = term if acc is None else acc + term
    return acc


def _interleave(streams, weights):
    results = [None] * len(streams)
    live = list(range(len(streams)))
    while live:
        for i in list(live):
            for _ in range(weights[i]):
                try:
                    next(streams[i])
                except StopIteration as stop:
                    results[i] = stop.value
                    live.remove(i)
                    break
    return results


def _const_spec(shape):
    nd = len(shape)
    return pl.BlockSpec(shape, lambda b, l: (0,) * nd, pipeline_mode=pl.Buffered(1))


def _tile_spec(tile, width):
    return pl.BlockSpec((1, tile, width), lambda b, l: (b, l, 0))


def _params():
    return pltpu.CompilerParams(dimension_semantics=("arbitrary", "arbitrary"),
                                vmem_limit_bytes=VMEM_LIMIT)


def _ffn_kernel(x_ref, gain_ref, wa_ref, wg_ref, cwa_ref, cwg_ref, cba_ref, cbg_ref, wd_ref,
                o_ref, ua_ref, ug_ref, act_ref):
    tile = x_ref.shape[1]
    blocks = FFN_COLS // LANES

    @pl.when(pl.program_id(1) == 0)
    def _():
        ua_ref[:, 0:SUBLANES, :] = jnp.zeros((FFN_DIM // LANES, SUBLANES, LANES), F32)
        ug_ref[:, 0:SUBLANES, :] = jnp.zeros((FFN_DIM // LANES, SUBLANES, LANES), F32)

    x = x_ref[0]
    h = _rms_rows(x, gain_ref[...]).astype(BF16)
    for c in range(FFN_DIM // FFN_COLS):
        sl = slice(c * FFN_COLS, (c + 1) * FFN_COLS)
        ua = _dot(h, wa_ref[:, sl])
        ug = _dot(h, wg_ref[:, sl])
        for t in range(blocks):
            j = c * blocks + t
            lanes = slice(j * LANES, (j + 1) * LANES)
            ua_ref[j, SUBLANES:SUBLANES + tile, :] = ua[:, t * LANES:(t + 1) * LANES]
            ug_ref[j, SUBLANES:SUBLANES + tile, :] = ug[:, t * LANES:(t + 1) * LANES]
            a = _causal_conv(ua_ref, j, cwa_ref, lanes, tile, cba_ref[:, lanes])
            g = _causal_conv(ug_ref, j, cwg_ref, lanes, tile, cbg_ref[:, lanes])
            act_ref[:, lanes] = (_silu(g) * a).astype(BF16)
    ua_ref[:, 0:SUBLANES, :] = ua_ref[:, tile:tile + SUBLANES, :]
    ug_ref[:, 0:SUBLANES, :] = ug_ref[:, tile:tile + SUBLANES, :]
    o_ref[0] = x + _dot(act_ref[...], wd_ref[...])


def _layer_spec(rows, cols, layer, col_block=0):
    return pl.BlockSpec((None, rows, cols), lambda b, l: (layer, 0, col_block),
                        pipeline_mode=pl.Buffered(1))


def _ffn_layer(x, layer, gains, w_up, conv_w, conv_b, w_down):
    bsz, seq, _ = x.shape
    tile = min(FFN_TILE, seq)
    args = (x, gains, w_up, w_up, conv_w, conv_w, conv_b, conv_b, w_down)
    in_specs = [_tile_spec(tile, D_MODEL),
                _layer_spec(1, D_MODEL, layer),
                _layer_spec(D_MODEL, FFN_DIM, layer, 0), _layer_spec(D_MODEL, FFN_DIM, layer, 1),
                _layer_spec(FFN_CONV, FFN_DIM, layer, 0), _layer_spec(FFN_CONV, FFN_DIM, layer, 1),
                _layer_spec(1, FFN_DIM, layer, 0), _layer_spec(1, FFN_DIM, layer, 1),
                _layer_spec(FFN_DIM, D_MODEL, layer)]
    return pl.pallas_call(
        _ffn_kernel,
        grid=(bsz, seq // tile),
        in_specs=in_specs,
        out_specs=_tile_spec(tile, D_MODEL),
        out_shape=jax.ShapeDtypeStruct(x.shape, F32),
        scratch_shapes=[pltpu.VMEM((FFN_DIM // LANES, SUBLANES + tile, LANES), F32),
                        pltpu.VMEM((FFN_DIM // LANES, SUBLANES + tile, LANES), F32),
                        pltpu.VMEM((tile, FFN_DIM), BF16)],
        compiler_params=_params(),
        name="conv_ffn",
    )(*args)


def _gla_steps(h, tile, wmain_ref, wglr_ref, wgate_ref, bgate_ref, glan_ref, seg_ref, tril_ref,
               st_ref, oa_ref):
    yield MXU
    proj = _dot(h, wmain_ref[:, 0:4 * MIX_W])
    gq = proj[:, 0 * MIX_W:1 * MIX_W]
    gk = proj[:, 1 * MIX_W:2 * MIX_W]
    gv = proj[:, 2 * MIX_W:3 * MIX_W]
    gr = proj[:, 3 * MIX_W:4 * MIX_W]
    glr = _dot(h, wglr_ref[...])
    gate = _dot(glr.astype(BF16), wgate_ref[...]) + bgate_ref[...]
    yield VPU
    log_g = (jnp.minimum(gate, 0.0) - jnp.log1p(jnp.exp(-jnp.abs(gate)))) * (1.0 / GLA_GATE_NORM)
    yield MXU
    cum = _dot01_lhs(tril_ref[...], log_g)
    yield VPU
    q_pos = (gq * HEAD_DIM ** -0.5) * jnp.exp(cum)
    k_neg = gk * jnp.exp(-cum)
    row = lax.broadcasted_iota(jnp.int32, (CHUNK, PAIR_W), 0)
    lane = lax.broadcasted_iota(jnp.int32, (CHUNK, PAIR_W), 1)
    causal = row >= (lane % CHUNK)
    same_head = _same_head_mask()
    rows = [slice(c * CHUNK, (c + 1) * CHUNK) for c in range(tile // CHUNK)]
    lanes = [slice(p * PAIR_W, (p + 1) * PAIR_W) for p in range(PAIRS)]
    cum_last = [cum[rs][CHUNK - 1:CHUNK] for rs in rows]
    k_dec = jnp.concatenate([gk[rs] * jnp.exp(cl - cum[rs]) for rs, cl in zip(rows, cum_last)], axis=0)
    q_b = q_pos.astype(BF16)
    kn_b = k_neg.astype(BF16)
    kd_b = k_dec.astype(BF16)
    v_b = gv.astype(BF16)
    scores, o_intra, upd = {}, {}, {}
    for c, rs in enumerate(rows):
        yield MXU
        for p, ls in enumerate(lanes):
            s = _dot_nt(q_b[rs, ls], _block_diag_pair(kn_b[rs, ls]))
            scores[c, p] = jnp.where(causal, s, 0.0).astype(BF16)
    for c, rs in enumerate(rows):
        yield MXU
        for p, ls in enumerate(lanes):
            upd[c, p] = jnp.where(same_head, _dot_tn(v_b[rs, ls], kd_b[rs, ls]), 0.0)
    for c, rs in enumerate(rows):
        yield MXU
        for p, ls in enumerate(lanes):
            o_intra[c, p] = _dot(scores[c, p], _block_diag_pair(v_b[rs, ls]))
    yield VPU
    states = {}
    for p, ls in enumerate(lanes):
        state_t = st_ref[p]
        for c, cl in enumerate(cum_last):
            states[c, p] = state_t
            state_t = state_t * jnp.exp(cl[:, ls]) + upd[c, p]
        st_ref[p] = state_t
    for c, rs in enumerate(rows):
        yield MXU
        for p, ls in enumerate(lanes):
            oa_ref[rs, ls] = o_intra[c, p] + _dot_nt(q_b[rs, ls], states[c, p].astype(BF16))
    yield MXU
    o_a = oa_ref[...]
    ms = _seg_mean(o_a * o_a, seg_ref[...])
    yield VPU
    return (o_a * lax.rsqrt(ms + NORM_EPS) * glan_ref[...] * _silu(gr)).astype(BF16)


def _swa_steps(h, tile, seq_tile, sinks_ref, wmain_ref, wkv_ref, qn_ref, kn_ref, seg_ref, bias_ref,
               kvs_ref, ob_ref):
    yield MXU
    sq = _dot(h, wmain_ref[:, 4 * MIX_W:5 * MIX_W])
    kv = _dot(h, wkv_ref[...])
    seg = seg_ref[...]
    yield MXU
    ms_q = _seg_mean(sq * sq, seg)
    sk = kv[:, 0:PAIR_W]
    ms_k = _seg_mean(sk * sk, seg[0:PAIR_W, 0:PAIR_W])
    yield VPU
    qn = (sq * lax.rsqrt(ms_q + NORM_EPS) * qn_ref[...] * HEAD_DIM ** -0.5).astype(BF16)
    kvs_ref[SWA_BLOCK:SWA_BLOCK + tile, 0:PAIR_W] = sk * lax.rsqrt(ms_k + NORM_EPS) * kn_ref[...]
    kvs_ref[SWA_BLOCK:SWA_BLOCK + tile, PAIR_W:2 * PAIR_W] = kv[:, PAIR_W:2 * PAIR_W]
    skey = lax.broadcasted_iota(jnp.int32, (2 * SWA_BLOCK, 2 * SWA_BLOCK), 1)
    top_col = lax.broadcasted_iota(jnp.int32, (2 * SWA_BLOCK, 1), 0) < SWA_BLOCK
    ones_cols = jnp.ones((2 * SWA_BLOCK, PAIR_W), BF16)
    first_lane = lax.broadcasted_iota(jnp.int32, (2 * SWA_BLOCK, PAIR_W), 1) < HEAD_DIM
    no_prev = jnp.where(skey < SWA_BLOCK, jnp.where(seq_tile > 0, 0.0, -jnp.inf), 0.0)
    for j in range(tile // SWA_BLOCK):
        ks = slice(j * SWA_BLOCK, (j + 2) * SWA_BLOCK)
        keys = kvs_ref[ks, 0:PAIR_W].astype(BF16)
        vals = kvs_ref[ks, PAIR_W:2 * PAIR_W].astype(BF16)
        zero_v = jnp.zeros_like(vals)
        vals0 = jnp.concatenate([jnp.where(first_lane, vals, zero_v), ones_cols], axis=1)
        vals1 = jnp.concatenate([jnp.where(first_lane, zero_v, vals), ones_cols], axis=1)
        for p in range(PAIRS):
            yield MXU
            qs = _block_diag_pair(qn[j * SWA_BLOCK:(j + 1) * SWA_BLOCK, p * PAIR_W:(p + 1) * PAIR_W])
            s = _dot_nt(qs, keys)
            yield VPU
            s = s + bias_ref[p]
            if j == 0:
                s = s + no_prev
            sink = jnp.where(top_col, sinks_ref[p], sinks_ref[p + PAIRS])
            m = jnp.maximum(jnp.max(s, axis=-1, keepdims=True), sink)
            pe = jnp.exp(s - m).astype(BF16)
            sink_e = jnp.exp(sink - m)
            yield MXU
            r0 = _dot(pe[0:SWA_BLOCK], vals0)
            r1 = _dot(pe[SWA_BLOCK:], vals1)
            o_pair = (r0[:, 0:PAIR_W] / (r0[:, PAIR_W:] + sink_e[0:SWA_BLOCK])
                      + r1[:, 0:PAIR_W] / (r1[:, PAIR_W:] + sink_e[SWA_BLOCK:]))
            ob_ref[j * SWA_BLOCK:(j + 1) * SWA_BLOCK, p * PAIR_W:(p + 1) * PAIR_W] = o_pair.astype(BF16)
    kvs_ref[0:SWA_BLOCK, :] = kvs_ref[tile:tile + SWA_BLOCK, :]
    return ob_ref[...]


def _even_kernel(sinks_ref, x_ref, gain_ref, wmain_ref, wkv_ref, wglr_ref, wgate_ref, bgate_ref,
                 glan_ref, qn_ref, kn_ref, seg_ref, tril_ref, bias_ref, woa_ref, wob_ref,
                 o_ref, st_ref, kvs_ref, oa_ref, ob_ref):
    tile = x_ref.shape[1]
    seq_tile = pl.program_id(1)

    @pl.when(seq_tile == 0)
    def _():
        st_ref[...] = jnp.zeros_like(st_ref)
        kvs_ref[0:SWA_BLOCK, :] = jnp.zeros((SWA_BLOCK, 2 * PAIR_W), F32)

    x = x_ref[0]
    h = _rms_rows(x, gain_ref[...]).astype(BF16)
    o_a, o_b = _interleave(
        [_gla_steps(h, tile, wmain_ref, wglr_ref, wgate_ref, bgate_ref, glan_ref, seg_ref, tril_ref,
                    st_ref, oa_ref),
         _swa_steps(h, tile, seq_tile, sinks_ref, wmain_ref, wkv_ref, qn_ref, kn_ref, seg_ref, bias_ref,
                    kvs_ref, ob_ref)],
        EVEN_WEIGHTS)
    o_ref[0] = x + _dot(o_a, woa_ref[...]) + _dot(o_b, wob_ref[...])


def _block_tril(tile):
    r = jnp.arange(tile)
    same = (r[:, None] // CHUNK) == (r[None, :] // CHUNK)
    return (same & (r[:, None] >= r[None, :])).astype(BF16)


def _block_ones(tile):
    r = jnp.arange(tile)
    return ((r[:, None] // CHUNK) == (r[None, :] // CHUNK)).astype(BF16)


def _seg_matrix():
    r = jnp.arange(MIX_W)
    return ((r[:, None] // HEAD_DIM) == (r[None, :] // HEAD_DIM)).astype(BF16)


def _swa_bias():
    r = jnp.arange(2 * SWA_BLOCK)
    dist = (r[:, None] % SWA_BLOCK) + SWA_BLOCK - r[None, :]
    valid = (dist >= 0) & (dist < SWA_BLOCK)
    head = jnp.arange(PAIRS)[:, None, None] + jnp.where(r < SWA_BLOCK, 0, PAIRS)[None, :, None]
    slope = jnp.exp2(-(head + 1).astype(F32))
    return jnp.where(valid[None], -slope * dist[None].astype(F32), -jnp.inf)


def _head_tile(v, reps):
    return jnp.tile(v.astype(F32), reps).reshape(1, reps * v.shape[0])


def _even_layer(x, gain, w_in, w_gate, b_gate, gla_norm, q_norm, k_norm, sinks, w_out):
    bsz, seq, _ = x.shape
    tile = min(2 * MIX_TILE, seq)
    order = jnp.array([h for p in range(PAIRS) for h in (p, p + PAIRS)])
    c0 = 4 * MIX_W
    glr0 = c0
    sq0 = c0 + GLA_RANK
    sk0 = sq0 + MIX_W
    w_sq = w_in[:, sq0:sk0].reshape(D_MODEL, HEADS, HEAD_DIM)[:, order].reshape(D_MODEL, MIX_W)
    w_main = jnp.concatenate([w_in[:, :c0], w_sq], axis=1).astype(BF16)
    w_kv = w_in[:, sk0:].astype(BF16)
    w_glr = jnp.pad(w_in[:, glr0:sq0], ((0, 0), (0, LANES - GLA_RANK))).astype(BF16)
    w_gate_p = jnp.pad(w_gate, ((0, LANES - GLA_RANK), (0, 0))).astype(BF16)
    w_out_a = w_out[:MIX_W].astype(BF16)
    w_out_b = w_out[MIX_W:].reshape(HEADS, HEAD_DIM, D_MODEL)[order].reshape(MIX_W, D_MODEL).astype(BF16)
    args = (x, gain.reshape(1, D_MODEL), w_main, w_kv, w_glr, w_gate_p, b_gate.reshape(1, MIX_W),
            _head_tile(gla_norm, HEADS), _head_tile(q_norm, HEADS), _head_tile(k_norm, SWA_KV_HEADS),
            _seg_matrix(), _block_tril(tile), _swa_bias(), w_out_a, w_out_b)
    in_specs = ([pl.BlockSpec(memory_space=pltpu.SMEM), _tile_spec(tile, D_MODEL)]
                + [_const_spec(a.shape) for a in args[1:]])
    return pl.pallas_call(
        _even_kernel,
        grid=(bsz, seq // tile),
        in_specs=in_specs,
        out_specs=_tile_spec(tile, D_MODEL),
        out_shape=jax.ShapeDtypeStruct(x.shape, F32),
        scratch_shapes=[pltpu.VMEM((PAIRS, PAIR_W, PAIR_W), F32),
                        pltpu.VMEM((SWA_BLOCK + tile, 2 * PAIR_W), F32),
                        pltpu.VMEM((tile, MIX_W), F32),
                        pltpu.VMEM((tile, MIX_W), BF16)],
        compiler_params=_params(),
        name="gla_swa_mixer",
    )(sinks.astype(F32), *args)


def _s5_prep_kernel(are_ref, aim_ref, ldt_ref, bre_ref, bim_ref, pre_ref, pim_ref, bbre_ref, bbim_ref):
    a_re = are_ref[...]
    a_im = aim_ref[...]
    dt = jnp.exp(ldt_ref[...])
    for k in range(SCAN_LEVELS):
        step = float(2 ** k)
        mag = jnp.exp(step * dt * a_re)
        ang = step * dt * a_im
        pre_ref[k] = mag * jnp.cos(ang)
        pim_ref[k] = mag * jnp.sin(ang)
    abar_re = pre_ref[0]
    abar_im = pim_ref[0]
    den = a_re * a_re + a_im * a_im
    f_re = ((abar_re - 1.0) * a_re + abar_im * a_im) / den
    f_im = (abar_im * a_re - (abar_re - 1.0) * a_im) / den
    for c in range(S5_GROUP):
        bbre_ref[c] = f_re * bre_ref[c] - f_im * bim_ref[c]
        bbim_ref[c] = f_re * bim_ref[c] + f_im * bre_ref[c]


def _s5_steps(h, tile, wsu_ref, bre_ref, bim_ref, pre_ref, pim_ref, cre_ref, cim_ref,
              dskip_ref, wglu_ref, bglu_ref, buf_ref, st_ref, xs_ref):
    half_in = MIX_W // 2
    half_blocks = S5_LANE_BLOCKS // 2
    body = slice(SUBLANES, SUBLANES + tile)
    tail = slice(tile, tile + SUBLANES)
    head = slice(0, SUBLANES)
    yield MXU
    u = _dot(h, wsu_ref[...])
    ub = u.astype(BF16)
    for kt in range(2):
        yield MXU
        u_half = ub[:, kt * half_in:(kt + 1) * half_in]
        bu_re = _dot(u_half, bre_ref[kt])
        bu_im = _dot(u_half, bim_ref[kt])
        for j in range(half_blocks):
            lb = kt * half_blocks + j
            buf_ref[0, 0, lb, body, :] = bu_re[:, j * LANES:(j + 1) * LANES]
            buf_ref[0, 1, lb, body, :] = bu_im[:, j * LANES:(j + 1) * LANES]

    def scan_blocks(blocks):
        for lb in blocks:
            yield VPU
            p_re = pre_ref[lb]
            p_im = pim_ref[lb]
            for k, (src, dst) in enumerate(((0, 1), (1, 2), (2, 1))):
                shift = 2 ** k
                shifted = slice(SUBLANES - shift, SUBLANES - shift + tile)
                cur_re = buf_ref[src, 0, lb, body, :]
                cur_im = buf_ref[src, 1, lb, body, :]
                sh_re = buf_ref[src, 0, lb, shifted, :]
                sh_im = buf_ref[src, 1, lb, shifted, :]
                a_re = p_re[k:k + 1]
                a_im = p_im[k:k + 1]
                new_re = cur_re + a_re * sh_re - a_im * sh_im
                new_im = cur_im + a_re * sh_im + a_im * sh_re
                buf_ref[src, 0, lb, head, :] = buf_ref[src, 0, lb, tail, :]
                buf_ref[src, 1, lb, head, :] = buf_ref[src, 1, lb, tail, :]
                buf_ref[dst, 0, lb, body, :] = new_re
                buf_ref[dst, 1, lb, body, :] = new_im
        a8 = [(pre_ref[lb][3:4], pim_ref[lb][3:4]) for lb in blocks]
        state = [(st_ref[0, lb], st_ref[1, lb]) for lb in blocks]
        groups = tile // SUBLANES
        for g in range(groups):
            if g % (groups // 4) == 0:
                yield VPU
            grp = slice(SUBLANES * (g + 1), SUBLANES * (g + 2))
            for n, lb in enumerate(blocks):
                a_re, a_im = a8[n]
                s_re, s_im = state[n]
                x_re = buf_ref[1, 0, lb, grp, :] + a_re * s_re - a_im * s_im
                x_im = buf_ref[1, 1, lb, grp, :] + a_re * s_im + a_im * s_re
                buf_ref[1, 0, lb, grp, :] = x_re
                buf_ref[1, 1, lb, grp, :] = x_im
                state[n] = (x_re, x_im)
        yield VPU
        for n, lb in enumerate(blocks):
            st_ref[0, lb] = state[n][0]
            st_ref[1, lb] = state[n][1]
            xs_ref[0, lb] = buf_ref[1, 0, lb, body, :].astype(BF16)
            xs_ref[1, lb] = buf_ref[1, 1, lb, body, :].astype(BF16)

    ys = []
    for kt in range(2):
        blocks = list(range(kt * half_blocks, (kt + 1) * half_blocks))
        for n in range(0, half_blocks, SCAN_UNROLL):
            yield from scan_blocks(blocks[n:n + SCAN_UNROLL])
        yield MXU
        x_re = jnp.concatenate([xs_ref[0, lb] for lb in blocks], axis=1)
        x_im = jnp.concatenate([xs_ref[1, lb] for lb in blocks], axis=1)
        ys.append(_dot(x_re, cre_ref[kt]) - _dot(x_im, cim_ref[kt]))
    yield VPU
    y = jnp.concatenate(ys, axis=1) + dskip_ref[...] * u
    y = 0.5 * y * (1.0 + jnp.tanh(math.sqrt(2.0 / math.pi) * (y + 0.044715 * (y * y * y))))
    yield MXU
    gate = _dot(y.astype(BF16), wglu_ref[...]) + bglu_ref[...]
    yield VPU
    return (y * _sigmoid(gate)).astype(BF16)


def _gdn_steps(h, tile, wqkvz_ref, wda_ref, wdb_ref, cw_ref, alog_ref, dtb_ref, gnorm_ref, seg_ref,
               tril_ref, onesbd_ref, st_ref, cs_ref, od_ref, attn_ref, uw_ref, ku_ref, kwq_ref):
    qkv_w = 3 * MIX_W
    yield MXU
    da = _dot(h, wda_ref[...])
    db = _dot(h, wdb_ref[...])
    yield VPU
    log_alpha = -jnp.exp(alog_ref[...]) * _softplus(da + dtb_ref[...])
    beta = _sigmoid(db)
    row = lax.broadcasted_iota(jnp.int32, (tile, MIX_W), 0) % CHUNK
    key = lax.broadcasted_iota(jnp.int32, (tile, MIX_W), 1) % CHUNK
    yield MXU
    g = _dot01_lhs(tril_ref[...], log_alpha)
    yield MXU
    g_key = _dot01_lhs(onesbd_ref[...], jnp.where(row <= key, log_alpha, 0.0))
    yield MXU
    proj = _dot(h, wqkvz_ref[:, 0:qkv_w])
    yield VPU
    decay_incl = jnp.exp(jnp.where(row >= key, g - g_key, -jnp.inf))
    decay_strict = jnp.where(row > key, decay_incl, 0.0)
    exp_g = jnp.exp(g)
    conv = []
    for j in range(qkv_w // LANES):
        if j % 4 == 0:
            yield VPU
        lanes_j = slice(j * LANES, (j + 1) * LANES)
        cs_ref[j, SUBLANES:SUBLANES + tile, :] = proj[:, lanes_j]
        conv.append(_silu(_causal_conv(cs_ref, j, cw_ref, lanes_j, tile)))
    cs_ref[:, 0:SUBLANES, :] = cs_ref[:, tile:tile + SUBLANES, :]
    qkv = jnp.concatenate(conv, axis=1)
    q = qkv[:, 0:MIX_W]
    k_ = qkv[:, MIX_W:2 * MIX_W]
    v = qkv[:, 2 * MIX_W:]
    yield MXU
    seg = seg_ref[...]
    ss_q = _seg_mean(q * q, seg) * HEAD_DIM
    ss_k = _seg_mean(k_ * k_, seg) * HEAD_DIM
    z = _dot(h, wqkvz_ref[:, qkv_w:])
    yield VPU
    q = q * lax.rsqrt(ss_q + NORM_EPS) * HEAD_DIM ** -0.5
    k_ = k_ * lax.rsqrt(ss_k + NORM_EPS)
    rows = [slice(c * CHUNK, (c + 1) * CHUNK) for c in range(tile // CHUNK)]
    lanes = [slice(p * PAIR_W, (p + 1) * PAIR_W) for p in range(PAIRS)]
    g_last = [g[rs][CHUNK - 1:CHUNK] for rs in rows]
    k_dec = jnp.concatenate([k_[rs] * jnp.exp(gl - g[rs]) for rs, gl in zip(rows, g_last)], axis=0)
    k_b = k_.astype(BF16)
    q_b = q.astype(BF16)
    kd_b = k_dec.astype(BF16)
    qd_b = (q * exp_g).astype(BF16)
    vb_b = (v * beta).astype(BF16)
    kbg_b = (k_ * beta * exp_g).astype(BF16)
    eye_pair = (lax.broadcasted_iota(jnp.int32, (CHUNK, PAIR_W), 0)
                == lax.broadcasted_iota(jnp.int32, (CHUNK, PAIR_W), 1) % CHUNK).astype(F32)
    same_head = _same_head_mask()
    first_lane = lax.broadcasted_iota(jnp.int32, (CHUNK, PAIR_W), 1) < HEAD_DIM
    zero = jnp.zeros((CHUNK, PAIR_W), BF16)
    power, power_bd, inv = {}, {}, {}
    for first in range(0, len(rows), GDN_GROUP):
        group = range(first, first + GDN_GROUP)
        for c in group:
            yield MXU
            rs = rows[c]
            for p, ls in enumerate(lanes):
                sc = _dot_nt(jnp.concatenate([k_b[rs, ls], q_b[rs, ls]], axis=0), _block_diag_pair(k_b[rs, ls]))
                lower = beta[rs, ls] * sc[0:CHUNK] * decay_strict[rs, ls]
                power[c, p] = lower.astype(BF16)
                power_bd[c, p] = _block_diag_pair(power[c, p])
                inv[c, p] = eye_pair - lower
                attn_ref[rs, ls] = (sc[CHUNK:] * decay_incl[rs, ls]).astype(BF16)
        for _ in range(int(math.log2(CHUNK)) - 1):
            for c in group:
                yield MXU
                for p in range(PAIRS):
                    power[c, p] = _dot(power[c, p], power_bd[c, p]).astype(BF16)
                    power_bd[c, p] = _block_diag_pair(power[c, p])
            for c in group:
                yield MXU
                for p in range(PAIRS):
                    inv[c, p] = inv[c, p] + _dot(inv[c, p].astype(BF16), power_bd[c, p])
        for c in group:
            yield MXU
            rs = rows[c]
            for p, ls in enumerate(lanes):
                vb = vb_b[rs, ls]
                kbg = kbg_b[rs, ls]
                rhs = jnp.concatenate(
                    [jnp.concatenate([jnp.where(first_lane, vb, zero), jnp.where(first_lane, kbg, zero)], axis=1),
                     jnp.concatenate([jnp.where(first_lane, zero, vb), jnp.where(first_lane, zero, kbg)], axis=1)],
                    axis=0)
                uw_ref[c, p] = _dot(inv[c, p].astype(BF16), rhs).astype(BF16)
    for c, rs in enumerate(rows):
        yield MXU
        for p, ls in enumerate(lanes):
            uw = uw_ref[c, p]
            w_b = uw[:, PAIR_W:]
            kw_ku = _dot_tn(kd_b[rs, ls], jnp.concatenate([w_b, uw[:, 0:PAIR_W]], axis=1))
            aw_au = _dot(attn_ref[rs, ls], jnp.concatenate(
                [_block_diag_pair(w_b), _block_diag_pair(uw[:, 0:PAIR_W])], axis=1))
            kw = jnp.where(same_head, kw_ku[:, 0:PAIR_W], 0.0).astype(BF16)
            ku_ref[c, p] = jnp.where(same_head, kw_ku[:, PAIR_W:], 0.0)
            q_eff = (qd_b[rs, ls].astype(F32) - aw_au[:, 0:PAIR_W]).astype(BF16)
            kwq_ref[c, p] = jnp.concatenate([kw, q_eff], axis=0)
            od_ref[rs, ls] = aw_au[:, PAIR_W:]
    for c, (rs, gl) in enumerate(zip(rows, g_last)):
        yield MXU
        state = [st_ref[p] for p in range(PAIRS)]
        prod = [_dot(kwq_ref[c, p], st.astype(BF16)) for p, st in enumerate(state)]
        for p, ls in enumerate(lanes):
            st_ref[p] = state[p] * jnp.exp(gl[:, ls]) - prod[p][0:PAIR_W] + ku_ref[c, p]
            od_ref[rs, ls] = od_ref[rs, ls] + prod[p][PAIR_W:]
    yield MXU
    o_d = od_ref[...]
    ms = _seg_mean(o_d * o_d, seg)
    yield VPU
    return (o_d * lax.rsqrt(ms + NORM_EPS) * gnorm_ref[...] * _silu(z)).astype(BF16)


def _odd_kernel(x_ref, gain_ref,
                wsu_ref, bre_ref, bim_ref, pre_ref, pim_ref, cre_ref, cim_ref, dskip_ref, wglu_ref, bglu_ref,
                wqkvz_ref, wda_ref, wdb_ref, cw_ref, alog_ref, dtb_ref, gnorm_ref, seg_ref, tril_ref,
                onesbd_ref, woc_ref, wod_ref,
                o_ref,
                buf_ref, sst_ref, xs_ref, gst_ref, cs_ref, od_ref, attn_ref, uw_ref, ku_ref, kwq_ref):
    tile = x_ref.shape[1]

    @pl.when(pl.program_id(1) == 0)
    def _():
        sst_ref[...] = jnp.zeros_like(sst_ref)
        buf_ref[:, :, :, 0:SUBLANES, :] = jnp.zeros((3, 2, S5_LANE_BLOCKS, SUBLANES, LANES), F32)
        gst_ref[...] = jnp.zeros_like(gst_ref)
        cs_ref[:, 0:SUBLANES, :] = jnp.zeros((3 * MIX_W // LANES, SUBLANES, LANES), F32)

    x = x_ref[0]
    h = _rms_rows(x, gain_ref[...]).astype(BF16)
    o_c, o_d = _interleave(
        [_s5_steps(h, tile, wsu_ref, bre_ref, bim_ref, pre_ref, pim_ref, cre_ref, cim_ref,
                   dskip_ref, wglu_ref, bglu_ref, buf_ref, sst_ref, xs_ref),
         _gdn_steps(h, tile, wqkvz_ref, wda_ref, wdb_ref, cw_ref, alog_ref, dtb_ref, gnorm_ref, seg_ref,
                    tril_ref, onesbd_ref, gst_ref, cs_ref, od_ref, attn_ref, uw_ref, ku_ref, kwq_ref)],
        ODD_WEIGHTS)
    o_ref[0] = x + _dot(o_c, woc_ref[...]) + _dot(o_d, wod_ref[...])


def _odd_layer(x, gain, w_in, a_re, a_im, log_dt, b_re, b_im, c_re, c_im, d_skip, w_glu, b_glu,
               conv_w, a_log, dt_bias, gdn_norm, w_out):
    bsz, seq, _ = x.shape
    tile = min(MIX_TILE, seq)
    su1 = MIX_W
    z1 = su1 + 4 * MIX_W
    gs = (S5_GROUPS, S5_STATE)
    pow_re, pow_im, bb_re, bb_im = pl.pallas_call(
        _s5_prep_kernel,
        out_shape=[jax.ShapeDtypeStruct((SCAN_LEVELS,) + gs, F32)] * 2
        + [jax.ShapeDtypeStruct((S5_GROUP,) + gs, F32)] * 2,
        name="s5_discretize",
    )(a_re, a_im, log_dt.reshape(S5_GROUPS, 1), b_re.transpose(2, 0, 1), b_im.transpose(2, 0, 1))
    half_g = S5_GROUPS // 2
    eye = jnp.eye(half_g, dtype=F32)

    def in_blocks(bb):
        t = bb.reshape(S5_GROUP, 2, half_g, S5_STATE)
        return jnp.einsum('ckgp,gh->kgchp', t, eye).reshape(2, half_g * S5_GROUP, half_g * S5_STATE).astype(BF16)

    def out_blocks(cc):
        t = cc.reshape(2, half_g, S5_GROUP, S5_STATE)
        return jnp.einsum('kgop,gh->kgpho', t, eye).reshape(2, half_g * S5_STATE, half_g * S5_GROUP).astype(BF16)

    def lane_blocks(pw):
        return pw.reshape(SCAN_LEVELS, S5_LANE_BLOCKS, LANES).transpose(1, 0, 2)

    rep = lambda w: jnp.repeat(w, HEAD_DIM, axis=-1)
    args = (x, gain.reshape(1, D_MODEL),
            w_in[:, :su1].astype(BF16), in_blocks(bb_re), in_blocks(bb_im),
            lane_blocks(pow_re), lane_blocks(pow_im), out_blocks(c_re), out_blocks(c_im),
            d_skip.reshape(1, MIX_W), w_glu.astype(BF16), b_glu.reshape(1, MIX_W),
            w_in[:, su1:z1].astype(BF16), rep(w_in[:, z1:z1 + HEADS]).astype(BF16),
            rep(w_in[:, z1 + HEADS:]).astype(BF16), conv_w, rep(a_log).reshape(1, MIX_W),
            rep(dt_bias).reshape(1, MIX_W), _head_tile(gdn_norm, HEADS), _seg_matrix(),
            _block_tril(tile), _block_ones(tile), w_out[:MIX_W].astype(BF16), w_out[MIX_W:].astype(BF16))
    in_specs = [_tile_spec(tile, D_MODEL)] + [_const_spec(a.shape) for a in args[1:]]
    return pl.pallas_call(
        _odd_kernel,
        grid=(bsz, seq // tile),
        in_specs=in_specs,
        out_specs=_tile_spec(tile, D_MODEL),
        out_shape=jax.ShapeDtypeStruct(x.shape, F32),
        scratch_shapes=[pltpu.VMEM((3, 2, S5_LANE_BLOCKS, SUBLANES + tile, LANES), F32),
                        pltpu.VMEM((2, S5_LANE_BLOCKS, SUBLANES, LANES), F32),
                        pltpu.VMEM((2, S5_LANE_BLOCKS, tile, LANES), BF16),
                        pltpu.VMEM((PAIRS, PAIR_W, PAIR_W), F32),
                        pltpu.VMEM((3 * MIX_W // LANES, SUBLANES + tile, LANES), F32),
                        pltpu.VMEM((tile, MIX_W), F32),
                        pltpu.VMEM((tile, MIX_W), BF16),
                        pltpu.VMEM((tile // CHUNK, PAIRS, CHUNK, 2 * PAIR_W), BF16),
                        pltpu.VMEM((tile // CHUNK, PAIRS, PAIR_W, PAIR_W), F32),
                        pltpu.VMEM((tile // CHUNK, PAIRS, PAIR_W + CHUNK, PAIR_W), BF16)],
        compiler_params=_params(),
        name="s5_gdn_mixer",
    )(*args)


def kernel(x, norm_mix, norm_ffn, w_in_even, w_gla_gate, b_gla_gate, gla_out_norm, swa_q_norm, swa_k_norm, swa_sinks, w_out_even, w_in_odd, s5_a_re, s5_a_im, s5_log_dt, s5_b_re, s5_b_im, s5_c_re, s5_c_im, s5_d, s5_w_glu, s5_b_glu, gdn_conv_w, gdn_a_log, gdn_dt_bias, gdn_out_norm, w_out_odd, w_ffn_up, ffn_conv_w, ffn_conv_b, w_ffn_down):
    depth = norm_mix.shape[0]
    ffn_gains = norm_ffn.reshape(depth, 1, D_MODEL)
    ffn_up = w_ffn_up.astype(BF16)
    ffn_down = w_ffn_down.astype(BF16)
    ffn_bias = ffn_conv_b.reshape(depth, 1, 2 * FFN_DIM)
    for layer in range(depth):
        i = layer // 2
        if layer % 2 == 0:
            x = _even_layer(x, norm_mix[layer], w_in_even[i], w_gla_gate[i], b_gla_gate[i],
                            gla_out_norm[i], swa_q_norm[i], swa_k_norm[i], swa_sinks[i], w_out_even[i])
        else:
            x = _odd_layer(x, norm_mix[layer], w_in_odd[i], s5_a_re[i], s5_a_im[i], s5_log_dt[i],
                           s5_b_re[i], s5_b_im[i], s5_c_re[i], s5_c_im[i], s5_d[i], s5_w_glu[i],
                           s5_b_glu[i], gdn_conv_w[i], gdn_a_log[i], gdn_dt_bias[i], gdn_out_norm[i],
                           w_out_odd[i])
        x = _ffn_layer(x, layer, ffn_gains, ffn_up, ffn_conv_w, ffn_bias, ffn_down)
    return x
```

```python
import math

import jax
import jax.numpy as jnp
from jax import lax
from jax.experimental import pallas as pl
from jax.experimental.pallas import tpu as pltpu

F32 = jnp.float32
BF16 = jnp.bfloat16

D_MODEL = 1024
HEAD_DIM = 64
HEADS = 8
MIX_W = HEADS * HEAD_DIM
PAIRS = HEADS // 2
PAIR_W = 2 * HEAD_DIM
GLA_RANK = 16
GLA_GATE_NORM = 16.0
CHUNK = 64
SWA_BLOCK = 128
SWA_KV_HEADS = 2
S5_GROUP = 16
S5_GROUPS = 32
S5_STATE = 64
S5_LANES = S5_GROUPS * S5_STATE
GDN_CONV = 4
FFN_DIM = 2816
FFN_CONV = 3
NORM_EPS = 1e-6
LANES = 128
SUBLANES = 8
MXU_COLS = 256
VMEM_LIMIT = 56 * 1024 * 1024
S5_LANE_BLOCKS = S5_LANES // LANES

MIX_TILE = 256
FFN_TILE = 512
FFN_COLS = MXU_COLS
SCAN_LEVELS = int(math.log2(SUBLANES)) + 1
SCAN_UNROLL = 8
MXU = "mxu"
VPU = "vpu"
EVEN_WEIGHTS = (1, 2)
ODD_WEIGHTS = (1, 1)
GDN_GROUP = 4


def _dot(a, b):
    return jnp.dot(a, b, preferred_element_type=F32)


def _dot_nt(a, b):
    return lax.dot_general(a, b, (((1,), (1,)), ((), ())), preferred_element_type=F32)


def _dot_tn(a, b):
    return lax.dot_general(a, b, (((0,), (0,)), ((), ())), preferred_element_type=F32)


def _dot01_lhs(m01, x):
    hi = x.astype(BF16)
    lo = (x - hi.astype(F32)).astype(BF16)
    return _dot(m01, hi) + _dot(m01, lo)


def _sigmoid(x):
    return 1.0 / (1.0 + jnp.exp(-x))


def _silu(x):
    return x * _sigmoid(x)


def _softplus(x):
    return jnp.maximum(x, 0.0) + jnp.log1p(jnp.exp(-jnp.abs(x)))


def _rms_rows(x, gain_row):
    ms = jnp.mean(x * x, axis=-1, keepdims=True)
    return x * lax.rsqrt(ms + NORM_EPS) * gain_row


def _seg_mean(y, seg01):
    return _dot(y.astype(BF16), seg01) * (1.0 / HEAD_DIM)


def _block_diag_pair(v):
    first = lax.broadcasted_iota(jnp.int32, v.shape, 1) < HEAD_DIM
    zero = jnp.zeros_like(v)
    return jnp.concatenate([jnp.where(first, v, zero), jnp.where(first, zero, v)], axis=0)


def _same_head_mask():
    r = lax.broadcasted_iota(jnp.int32, (PAIR_W, PAIR_W), 0)
    c = lax.broadcasted_iota(jnp.int32, (PAIR_W, PAIR_W), 1)
    return (r < HEAD_DIM) == (c < HEAD_DIM)


def _causal_conv(u_ref, j, taps_ref, lanes, tile, acc=None):
    taps = taps_ref.shape[0]
    for k in range(taps):
        off = SUBLANES - (taps - 1) + k
        term = taps_ref[k:k + 1, lanes] * u_ref[j, off:off + tile, :]
        acc = term if acc is None else acc + term
    return acc


def _interleave(streams, weights):
    results = [None] * len(streams)
    live = list(range(len(streams)))
    while live:
        for i in list(live):
            for _ in range(weights[i]):
                try:
                    next(streams[i])
                except StopIteration as stop:
                    results[i] = stop.value
                    live.remove(i)
                    break
    return results


def _const_spec(shape):
    nd = len(shape)
    return pl.BlockSpec(shape, lambda b, l: (0,) * nd, pipeline_mode=pl.Buffered(1))


def _tile_spec(tile, width):
    return pl.BlockSpec((1, tile, width), lambda b, l: (b, l, 0))


def _params():
    return pltpu.CompilerParams(dimension_semantics=("arbitrary", "arbitrary"),
                                vmem_limit_bytes=VMEM_LIMIT)


def _ffn_kernel(x_ref, gain_ref, wa_ref, wg_ref, cwa_ref, cwg_ref, cba_ref, cbg_ref, wd_ref,
                o_ref, ua_ref, ug_ref, act_ref):
    tile = x_ref.shape[1]
    blocks = FFN_COLS // LANES

    @pl.when(pl.program_id(1) == 0)
    def _():
        ua_ref[:, 0:SUBLANES, :] = jnp.zeros((FFN_DIM // LANES, SUBLANES, LANES), F32)
        ug_ref[:, 0:SUBLANES, :] = jnp.zeros((FFN_DIM // LANES, SUBLANES, LANES), F32)

    x = x_ref[0]
    h = _rms_rows(x, gain_ref[...]).astype(BF16)
    for c in range(FFN_DIM // FFN_COLS):
        sl = slice(c * FFN_COLS, (c + 1) * FFN_COLS)
        ua = _dot(h, wa_ref[:, sl])
        ug = _dot(h, wg_ref[:, sl])
        for t in range(blocks):
            j = c * blocks + t
            lanes = slice(j * LANES, (j + 1) * LANES)
            ua_ref[j, SUBLANES:SUBLANES + tile, :] = ua[:, t * LANES:(t + 1) * LANES]
            ug_ref[j, SUBLANES:SUBLANES + tile, :] = ug[:, t * LANES:(t + 1) * LANES]
            a = _causal_conv(ua_ref, j, cwa_ref, lanes, tile, cba_ref[:, lanes])
            g = _causal_conv(ug_ref, j, cwg_ref, lanes, tile, cbg_ref[:, lanes])
            act_ref[:, lanes] = (_silu(g) * a).astype(BF16)
    ua_ref[:, 0:SUBLANES, :] = ua_ref[:, tile:tile + SUBLANES, :]
    ug_ref[:, 0:SUBLANES, :] = ug_ref[:, tile:tile + SUBLANES, :]
    o_ref[0] = x + _dot(act_ref[...], wd_ref[...])


def _layer_spec(rows, cols, layer, col_block=0):
    return pl.BlockSpec((None, rows, cols), lambda b, l: (layer, 0, col_block),
                        pipeline_mode=pl.Buffered(1))


def _ffn_layer(x, layer, gains, w_up, conv_w, conv_b, w_down):
    bsz, seq, _ = x.shape
    tile = min(FFN_TILE, seq)
    args = (x, gains, w_up, w_up, conv_w, conv_w, conv_b, conv_b, w_down)
    in_specs = [_tile_spec(tile, D_MODEL),
                _layer_spec(1, D_MODEL, layer),
                _layer_spec(D_MODEL, FFN_DIM, layer, 0), _layer_spec(D_MODEL, FFN_DIM, layer, 1),
                _layer_spec(FFN_CONV, FFN_DIM, layer, 0), _layer_spec(FFN_CONV, FFN_DIM, layer, 1),
                _layer_spec(1, FFN_DIM, layer, 0), _layer_spec(1, FFN_DIM, layer, 1),
                _layer_spec(FFN_DIM, D_MODEL, layer)]
    return pl.pallas_call(
        _ffn_kernel,
        grid=(bsz, seq // tile),
        in_specs=in_specs,
        out_specs=_tile_spec(tile, D_MODEL),
        out_shape=jax.ShapeDtypeStruct(x.shape, F32),
        scratch_shapes=[pltpu.VMEM((FFN_DIM // LANES, SUBLANES + tile, LANES), F32),
                        pltpu.VMEM((FFN_DIM // LANES, SUBLANES + tile, LANES), F32),
                        pltpu.VMEM((tile, FFN_DIM), BF16)],
        compiler_params=_params(),
        name="conv_ffn",
    )(*args)


def _gla_steps(h, tile, wmain_ref, wglr_ref, wgate_ref, bgate_ref, glan_ref, seg_ref, tril_ref,
               st_ref, oa_ref):
    yield MXU
    proj = _dot(h, wmain_ref[:, 0:4 * MIX_W])
    gq = proj[:, 0 * MIX_W:1 * MIX_W]
    gk = proj[:, 1 * MIX_W:2 * MIX_W]
    gv = proj[:, 2 * MIX_W:3 * MIX_W]
    gr = proj[:, 3 * MIX_W:4 * MIX_W]
    glr = _dot(h, wglr_ref[...])
    gate = _dot(glr.astype(BF16), wgate_ref[...]) + bgate_ref[...]
    yield VPU
    log_g = (jnp.minimum(gate, 0.0) - jnp.log1p(jnp.exp(-jnp.abs(gate)))) * (1.0 / GLA_GATE_NORM)
    yield MXU
    cum = _dot01_lhs(tril_ref[...], log_g)
    yield VPU
    q_pos = (gq * HEAD_DIM ** -0.5) * jnp.exp(cum)
    k_neg = gk * jnp.exp(-cum)
    row = lax.broadcasted_iota(jnp.int32, (CHUNK, PAIR_W), 0)
    lane = lax.broadcasted_iota(jnp.int32, (CHUNK, PAIR_W), 1)
    causal = row >= (lane % CHUNK)
    same_head = _same_head_mask()
    rows = [slice(c * CHUNK, (c + 1) * CHUNK) for c in range(tile // CHUNK)]
    lanes = [slice(p * PAIR_W, (p + 1) * PAIR_W) for p in range(PAIRS)]
    cum_last = [cum[rs][CHUNK - 1:CHUNK] for rs in rows]
    k_dec = jnp.concatenate([gk[rs] * jnp.exp(cl - cum[rs]) for rs, cl in zip(rows, cum_last)], axis=0)
    q_b = q_pos.astype(BF16)
    kn_b = k_neg.astype(BF16)
    kd_b = k_dec.astype(BF16)
    v_b = gv.astype(BF16)
    scores, o_intra, upd = {}, {}, {}
    for c, rs in enumerate(rows):
        yield MXU
        for p, ls in enumerate(lanes):
            s = _dot_nt(q_b[rs, ls], _block_diag_pair(kn_b[rs, ls]))
            scores[c, p] = jnp.where(causal, s, 0.0).astype(BF16)
    for c, rs in enumerate(rows):
        yield MXU
        for p, ls in enumerate(lanes):
            upd[c, p] = jnp.where(same_head, _dot_tn(v_b[rs, ls], kd_b[rs, ls]), 0.0)
    for c, rs in enumerate(rows):
        yield MXU
        for p, ls in enumerate(lanes):
            o_intra[c, p] = _dot(scores[c, p], _block_diag_pair(v_b[rs, ls]))
    yield VPU
    states = {}
    for p, ls in enumerate(lanes):
        state_t = st_ref[p]
        for c, cl in enumerate(cum_last):
            states[c, p] = state_t
            state_t = state_t * jnp.exp(cl[:, ls]) + upd[c, p]
        st_ref[p] = state_t
    for c, rs in enumerate(rows):
        yield MXU
        for p, ls in enumerate(lanes):
            oa_ref[rs, ls] = o_intra[c, p] + _dot_nt(q_b[rs, ls], states[c, p].astype(BF16))
    yield MXU
    o_a = oa_ref[...]
    ms = _seg_mean(o_a * o_a, seg_ref[...])
    yield VPU
    return (o_a * lax.rsqrt(ms + NORM_EPS) * glan_ref[...] * _silu(gr)).astype(BF16)


def _swa_steps(h, tile, seq_tile, sinks_ref, wmain_ref, wkv_ref, qn_ref, kn_ref, seg_ref, bias_ref,
               kvs_ref, ob_ref):
    yield MXU
    sq = _dot(h, wmain_ref[:, 4 * MIX_W:5 * MIX_W])
    kv = _dot(h, wkv_ref[...])
    seg = seg_ref[...]
    yield MXU
    ms_q = _seg_mean(sq * sq, seg)
    sk = kv[:, 0:PAIR_W]
    ms_k = _seg_mean(sk * sk, seg[0:PAIR_W, 0:PAIR_W])
    yield VPU
    qn = (sq * lax.rsqrt(ms_q + NORM_EPS) * qn_ref[...] * HEAD_DIM ** -0.5).astype(BF16)
    kvs_ref[SWA_BLOCK:SWA_BLOCK + tile, 0:PAIR_W] = sk * lax.rsqrt(ms_k + NORM_EPS) * kn_ref[...]
    kvs_ref[SWA_BLOCK:SWA_BLOCK + tile, PAIR_W:2 * PAIR_W] = kv[:, PAIR_W:2 * PAIR_W]
    skey = lax.broadcasted_iota(jnp.int32, (2 * SWA_BLOCK, 2 * SWA_BLOCK), 1)
    top_col = lax.broadcasted_iota(jnp.int32, (2 * SWA_BLOCK, 1), 0) < SWA_BLOCK
    ones_cols = jnp.ones((2 * SWA_BLOCK, PAIR_W), BF16)
    first_lane = lax.broadcasted_iota(jnp.int32, (2 * SWA_BLOCK, PAIR_W), 1) < HEAD_DIM
    no_prev = jnp.where(skey < SWA_BLOCK, jnp.where(seq_tile > 0, 0.0, -jnp.inf), 0.0)
    for j in range(tile // SWA_BLOCK):
        ks = slice(j * SWA_BLOCK, (j + 2) * SWA_BLOCK)
        keys = kvs_ref[ks, 0:PAIR_W].astype(BF16)
        vals = kvs_ref[ks, PAIR_W:2 * PAIR_W].astype(BF16)
        zero_v = jnp.zeros_like(vals)
        vals0 = jnp.concatenate([jnp.where(first_lane, vals, zero_v), ones_cols], axis=1)
        vals1 = jnp.concatenate([jnp.where(first_lane, zero_v, vals), ones_cols], axis=1)
        for p in range(PAIRS):
            yield MXU
            qs = _block_diag_pair(qn[j * SWA_BLOCK:(j + 1) * SWA_BLOCK, p * PAIR_W:(p + 1) * PAIR_W])
            s = _dot_nt(qs, keys)
            yield VPU
            s = s + bias_ref[p]
            if j == 0:
                s = s + no_prev
            sink = jnp.where(top_col, sinks_ref[p], sinks_ref[p + PAIRS])
            m = jnp.maximum(jnp.max(s, axis=-1, keepdims=True), sink)
            pe = jnp.exp(s - m).astype(BF16)
            sink_e = jnp.exp(sink - m)
            yield MXU
            r0 = _dot(pe[0:SWA_BLOCK], vals0)
            r1 = _dot(pe[SWA_BLOCK:], vals1)
            o_pair = (r0[:, 0:PAIR_W] / (r0[:, PAIR_W:] + sink_e[0:SWA_BLOCK])
                      + r1[:, 0:PAIR_W] / (r1[:, PAIR_W:] + sink_e[SWA_BLOCK:]))
            ob_ref[j * SWA_BLOCK:(j + 1) * SWA_BLOCK, p * PAIR_W:(p + 1) * PAIR_W] = o_pair.astype(BF16)
    kvs_ref[0:SWA_BLOCK, :] = kvs_ref[tile:tile + SWA_BLOCK, :]
    return ob_ref[...]


def _even_kernel(sinks_ref, x_ref, gain_ref, wmain_ref, wkv_ref, wglr_ref, wgate_ref, bgate_ref,
                 glan_ref, qn_ref, kn_ref, seg_ref, tril_ref, bias_ref, woa_ref, wob_ref,
                 o_ref, st_ref, kvs_ref, oa_ref, ob_ref):
    tile = x_ref.shape[1]
    seq_tile = pl.program_id(1)

    @pl.when(seq_tile == 0)
    def _():
        st_ref[...] = jnp.zeros_like(st_ref)
        kvs_ref[0:SWA_BLOCK, :] = jnp.zeros((SWA_BLOCK, 2 * PAIR_W), F32)

    x = x_ref[0]
    h = _rms_rows(x, gain_ref[...]).astype(BF16)
    o_a, o_b = _interleave(
        [_gla_steps(h, tile, wmain_ref, wglr_ref, wgate_ref, bgate_ref, glan_ref, seg_ref, tril_ref,
                    st_ref, oa_ref),
         _swa_steps(h, tile, seq_tile, sinks_ref, wmain_ref, wkv_ref, qn_ref, kn_ref, seg_ref, bias_ref,
                    kvs_ref, ob_ref)],
        EVEN_WEIGHTS)
    o_ref[0] = x + _dot(o_a, woa_ref[...]) + _dot(o_b, wob_ref[...])


def _block_tril(tile):
    r = jnp.arange(tile)
    same = (r[:, None] // CHUNK) == (r[None, :] // CHUNK)
    return (same & (r[:, None] >= r[None, :])).astype(BF16)


def _block_ones(tile):
    r = jnp.arange(tile)
    return ((r[:, None] // CHUNK) == (r[None, :] // CHUNK)).astype(BF16)


def _seg_matrix():
    r = jnp.arange(MIX_W)
    return ((r[:, None] // HEAD_DIM) == (r[None, :] // HEAD_DIM)).astype(BF16)


def _swa_bias():
    r = jnp.arange(2 * SWA_BLOCK)
    dist = (r[:, None] % SWA_BLOCK) + SWA_BLOCK - r[None, :]
    valid = (dist >= 0) & (dist < SWA_BLOCK)
    head = jnp.arange(PAIRS)[:, None, None] + jnp.where(r < SWA_BLOCK, 0, PAIRS)[None, :, None]
    slope = jnp.exp2(-(head + 1).astype(F32))
    return jnp.where(valid[None], -slope * dist[None].astype(F32), -jnp.inf)


def _head_tile(v, reps):
    return jnp.tile(v.astype(F32), reps).reshape(1, reps * v.shape[0])


def _even_layer(x, gain, w_in, w_gate, b_gate, gla_norm, q_norm, k_norm, sinks, w_out):
    bsz, seq, _ = x.shape
    tile = min(2 * MIX_TILE, seq)
    order = jnp.array([h for p in range(PAIRS) for h in (p, p + PAIRS)])
    c0 = 4 * MIX_W
    glr0 = c0
    sq0 = c0 + GLA_RANK
    sk0 = sq0 + MIX_W
    w_sq = w_in[:, sq0:sk0].reshape(D_MODEL, HEADS, HEAD_DIM)[:, order].reshape(D_MODEL, MIX_W)
    w_main = jnp.concatenate([w_in[:, :c0], w_sq], axis=1).astype(BF16)
    w_kv = w_in[:, sk0:].astype(BF16)
    w_glr = jnp.pad(w_in[:, glr0:sq0], ((0, 0), (0, LANES - GLA_RANK))).astype(BF16)
    w_gate_p = jnp.pad(w_gate, ((0, LANES - GLA_RANK), (0, 0))).astype(BF16)
    w_out_a = w_out[:MIX_W].astype(BF16)
    w_out_b = w_out[MIX_W:].reshape(HEADS, HEAD_DIM, D_MODEL)[order].reshape(MIX_W, D_MODEL).astype(BF16)
    args = (x, gain.reshape(1, D_MODEL), w_main, w_kv, w_glr, w_gate_p, b_gate.reshape(1, MIX_W),
            _head_tile(gla_norm, HEADS), _head_tile(q_norm, HEADS), _head_tile(k_norm, SWA_KV_HEADS),
            _seg_matrix(), _block_tril(tile), _swa_bias(), w_out_a, w_out_b)
    in_specs = ([pl.BlockSpec(memory_space=pltpu.SMEM), _tile_spec(tile, D_MODEL)]
                + [_const_spec(a.shape) for a in args[1:]])
    return pl.pallas_call(
        _even_kernel,
        grid=(bsz, seq // tile),
        in_specs=in_specs,
        out_specs=_tile_spec(tile, D_MODEL),
        out_shape=jax.ShapeDtypeStruct(x.shape, F32),
        scratch_shapes=[pltpu.VMEM((PAIRS, PAIR_W, PAIR_W), F32),
                        pltpu.VMEM((SWA_BLOCK + tile, 2 * PAIR_W), F32),
                        pltpu.VMEM((tile, MIX_W), F32),
                        pltpu.VMEM((tile, MIX_W), BF16)],
        compiler_params=_params(),
        name="gla_swa_mixer",
    )(sinks.astype(F32), *args)


def _s5_prep_kernel(are_ref, aim_ref, ldt_ref, bre_ref, bim_ref, pre_ref, pim_ref, bbre_ref, bbim_ref):
    a_re = are_ref[...]
    a_im = aim_ref[...]
    dt = jnp.exp(ldt_ref[...])
    for k in range(SCAN_LEVELS):
        step = float(2 ** k)
        mag = jnp.exp(step * dt * a_re)
        ang = step * dt * a_im
        pre_ref[k] = mag * jnp.cos(ang)
        pim_ref[k] = mag * jnp.sin(ang)
    abar_re = pre_ref[0]
    abar_im = pim_ref[0]
    den = a_re * a_re + a_im * a_im
    f_re = ((abar_re - 1.0) * a_re + abar_im * a_im) / den
    f_im = (abar_im * a_re - (abar_re - 1.0) * a_im) / den
    for c in range(S5_GROUP):
        bbre_ref[c] = f_re * bre_ref[c] - f_im * bim_ref[c]
        bbim_ref[c] = f_re * bim_ref[c] + f_im * bre_ref[c]


def _s5_steps(h, tile, wsu_ref, bre_ref, bim_ref, pre_ref, pim_ref, cre_ref, cim_ref,
              dskip_ref, wglu_ref, bglu_ref, buf_ref, st_ref, xs_ref):
    half_in = MIX_W // 2
    half_blocks = S5_LANE_BLOCKS // 2
    body = slice(SUBLANES, SUBLANES + tile)
    tail = slice(tile, tile + SUBLANES)
    head = slice(0, SUBLANES)
    yield MXU
    u = _dot(h, wsu_ref[...])
    ub = u.astype(BF16)
    for kt in range(2):
        yield MXU
        u_half = ub[:, kt * half_in:(kt + 1) * half_in]
        per = MXU_COLS // LANES
        for c in range(half_blocks // per):
            cols = slice(c * MXU_COLS, (c + 1) * MXU_COLS)
            bu_re = _dot(u_half, bre_ref[kt, :, cols])
            bu_im = _dot(u_half, bim_ref[kt, :, cols])
            for t in range(per):
                lb = kt * half_blocks + c * per + t
                buf_ref[0, 0, lb, body, :] = bu_re[:, t * LANES:(t + 1) * LANES]
                buf_ref[0, 1, lb, body, :] = bu_im[:, t * LANES:(t + 1) * LANES]

    def scan_blocks(blocks):
        for lb in blocks:
            yield VPU
            p_re = pre_ref[lb]
            p_im = pim_ref[lb]
            for k, (src, dst) in enumerate(((0, 1), (1, 2), (2, 1))):
                shift = 2 ** k
                shifted = slice(SUBLANES - shift, SUBLANES - shift + tile)
                cur_re = buf_ref[src, 0, lb, body, :]
                cur_im = buf_ref[src, 1, lb, body, :]
                sh_re = buf_ref[src, 0, lb, shifted, :]
                sh_im = buf_ref[src, 1, lb, shifted, :]
                a_re = p_re[k:k + 1]
                a_im = p_im[k:k + 1]
                new_re = cur_re + a_re * sh_re - a_im * sh_im
                new_im = cur_im + a_re * sh_im + a_im * sh_re
                buf_ref[src, 0, lb, head, :] = buf_ref[src, 0, lb, tail, :]
                buf_ref[src, 1, lb, head, :] = buf_ref[src, 1, lb, tail, :]
                buf_ref[dst, 0, lb, body, :] = new_re
                buf_ref[dst, 1, lb, body, :] = new_im
        a8 = [(pre_ref[lb][3:4], pim_ref[lb][3:4]) for lb in blocks]
        state = [(st_ref[0, lb], st_ref[1, lb]) for lb in blocks]
        groups = tile // SUBLANES
        for g in range(groups):
            if g % (groups // 4) == 0:
                yield VPU
            grp = slice(SUBLANES * (g + 1), SUBLANES * (g + 2))
            for n, lb in enumerate(blocks):
                a_re, a_im = a8[n]
                s_re, s_im = state[n]
                x_re = buf_ref[1, 0, lb, grp, :] + a_re * s_re - a_im * s_im
                x_im = buf_ref[1, 1, lb, grp, :] + a_re * s_im + a_im * s_re
                buf_ref[1, 0, lb, grp, :] = x_re
                buf_ref[1, 1, lb, grp, :] = x_im
                state[n] = (x_re, x_im)
        yield VPU
        for n, lb in enumerate(blocks):
            st_ref[0, lb] = state[n][0]
            st_ref[1, lb] = state[n][1]
            xs_ref[0, lb] = buf_ref[1, 0, lb, body, :].astype(BF16)
            xs_ref[1, lb] = buf_ref[1, 1, lb, body, :].astype(BF16)

    ys = []
    for kt in range(2):
        blocks = list(range(kt * half_blocks, (kt + 1) * half_blocks))
        for n in range(0, half_blocks, SCAN_UNROLL):
            yield from scan_blocks(blocks[n:n + SCAN_UNROLL])
        yield MXU
        x_re = jnp.concatenate([xs_ref[0, lb] for lb in blocks], axis=1)
        x_im = jnp.concatenate([xs_ref[1, lb] for lb in blocks], axis=1)
        ys.append(_dot(x_re, cre_ref[kt]) - _dot(x_im, cim_ref[kt]))
    yield VPU
    y = jnp.concatenate(ys, axis=1) + dskip_ref[...] * u
    y = 0.5 * y * (1.0 + jnp.tanh(math.sqrt(2.0 / math.pi) * (y + 0.044715 * (y * y * y))))
    yield MXU
    gate = _dot(y.astype(BF16), wglu_ref[...]) + bglu_ref[...]
    yield VPU
    return (y * _sigmoid(gate)).astype(BF16)


def _gdn_steps(h, tile, wqkvz_ref, wda_ref, wdb_ref, cw_ref, alog_ref, dtb_ref, gnorm_ref, seg_ref,
               tril_ref, onesbd_ref, st_ref, cs_ref, od_ref, attn_ref, uw_ref, ku_ref, kwq_ref):
    qkv_w = 3 * MIX_W
    yield MXU
    da = _dot(h, wda_ref[...])
    db = _dot(h, wdb_ref[...])
    yield VPU
    log_alpha = -jnp.exp(alog_ref[...]) * _softplus(da + dtb_ref[...])
    beta = _sigmoid(db)
    row = lax.broadcasted_iota(jnp.int32, (tile, MIX_W), 0) % CHUNK
    key = lax.broadcasted_iota(jnp.int32, (tile, MIX_W), 1) % CHUNK
    yield MXU
    g = _dot01_lhs(tril_ref[...], log_alpha)
    yield MXU
    g_key = _dot01_lhs(onesbd_ref[...], jnp.where(row <= key, log_alpha, 0.0))
    yield MXU
    for c in range(qkv_w // MXU_COLS):
        chunk = _dot(h, wqkvz_ref[:, c * MXU_COLS:(c + 1) * MXU_COLS])
        for t in range(MXU_COLS // LANES):
            cs_ref[c * (MXU_COLS // LANES) + t, SUBLANES:SUBLANES + tile, :] = chunk[:, t * LANES:(t + 1) * LANES]
    yield VPU
    decay_incl = jnp.exp(jnp.where(row >= key, g - g_key, -jnp.inf))
    decay_strict = jnp.where(row > key, decay_incl, 0.0)
    exp_g = jnp.exp(g)
    conv = []
    for j in range(qkv_w // LANES):
        if j % 4 == 0:
            yield VPU
        lanes_j = slice(j * LANES, (j + 1) * LANES)
        conv.append(_silu(_causal_conv(cs_ref, j, cw_ref, lanes_j, tile)))
    cs_ref[:, 0:SUBLANES, :] = cs_ref[:, tile:tile + SUBLANES, :]
    qkv = jnp.concatenate(conv, axis=1)
    q = qkv[:, 0:MIX_W]
    k_ = qkv[:, MIX_W:2 * MIX_W]
    v = qkv[:, 2 * MIX_W:]
    yield MXU
    seg = seg_ref[...]
    ss_q = _seg_mean(q * q, seg) * HEAD_DIM
    ss_k = _seg_mean(k_ * k_, seg) * HEAD_DIM
    z = _dot(h, wqkvz_ref[:, qkv_w:])
    yield VPU
    q = q * lax.rsqrt(ss_q + NORM_EPS) * HEAD_DIM ** -0.5
    k_ = k_ * lax.rsqrt(ss_k + NORM_EPS)
    rows = [slice(c * CHUNK, (c + 1) * CHUNK) for c in range(tile // CHUNK)]
    lanes = [slice(p * PAIR_W, (p + 1) * PAIR_W) for p in range(PAIRS)]
    g_last = [g[rs][CHUNK - 1:CHUNK] for rs in rows]
    k_dec = jnp.concatenate([k_[rs] * jnp.exp(gl - g[rs]) for rs, gl in zip(rows, g_last)], axis=0)
    k_b = k_.astype(BF16)
    q_b = q.astype(BF16)
    kd_b = k_dec.astype(BF16)
    qd_b = (q * exp_g).astype(BF16)
    vb_b = (v * beta).astype(BF16)
    kbg_b = (k_ * beta * exp_g).astype(BF16)
    eye_pair = (lax.broadcasted_iota(jnp.int32, (CHUNK, PAIR_W), 0)
                == lax.broadcasted_iota(jnp.int32, (CHUNK, PAIR_W), 1) % CHUNK).astype(F32)
    same_head = _same_head_mask()
    first_lane = lax.broadcasted_iota(jnp.int32, (CHUNK, PAIR_W), 1) < HEAD_DIM
    zero = jnp.zeros((CHUNK, PAIR_W), BF16)
    power, power_bd, inv = {}, {}, {}
    for first in range(0, len(rows), GDN_GROUP):
        group = range(first, first + GDN_GROUP)
        for c in group:
            yield MXU
            rs = rows[c]
            for p, ls in enumerate(lanes):
                sc = _dot_nt(jnp.concatenate([k_b[rs, ls], q_b[rs, ls]], axis=0), _block_diag_pair(k_b[rs, ls]))
                lower = beta[rs, ls] * sc[0:CHUNK] * decay_strict[rs, ls]
                power[c, p] = lower.astype(BF16)
                power_bd[c, p] = _block_diag_pair(power[c, p])
                inv[c, p] = eye_pair - lower
                attn_ref[rs, ls] = (sc[CHUNK:] * decay_incl[rs, ls]).astype(BF16)
        for _ in range(int(math.log2(CHUNK)) - 1):
            for c in group:
                yield MXU
                for p in range(PAIRS):
                    power[c, p] = _dot(power[c, p], power_bd[c, p]).astype(BF16)
                    power_bd[c, p] = _block_diag_pair(power[c, p])
            for c in group:
                yield MXU
                for p in range(PAIRS):
                    inv[c, p] = inv[c, p] + _dot(inv[c, p].astype(BF16), power_bd[c, p])
        for c in group:
            yield MXU
            rs = rows[c]
            for p, ls in enumerate(lanes):
                vb = vb_b[rs, ls]
                kbg = kbg_b[rs, ls]
                rhs = jnp.concatenate(
                    [jnp.concatenate([jnp.where(first_lane, vb, zero), jnp.where(first_lane, kbg, zero)], axis=1),
                     jnp.concatenate([jnp.where(first_lane, zero, vb), jnp.where(first_lane, zero, kbg)], axis=1)],
                    axis=0)
                uw_ref[c, p] = _dot(inv[c, p].astype(BF16), rhs).astype(BF16)
    for c, rs in enumerate(rows):
        yield MXU
        for p, ls in enumerate(lanes):
            uw = uw_ref[c, p]
            w_b = uw[:, PAIR_W:]
            kw_ku = _dot_tn(kd_b[rs, ls], jnp.concatenate([w_b, uw[:, 0:PAIR_W]], axis=1))
            aw_au = _dot(attn_ref[rs, ls], jnp.concatenate(
                [_block_diag_pair(w_b), _block_diag_pair(uw[:, 0:PAIR_W])], axis=1))
            kw = jnp.where(same_head, kw_ku[:, 0:PAIR_W], 0.0).astype(BF16)
            ku_ref[c, p] = jnp.where(same_head, kw_ku[:, PAIR_W:], 0.0)
            q_eff = (qd_b[rs, ls].astype(F32) - aw_au[:, 0:PAIR_W]).astype(BF16)
            kwq_ref[c, p] = jnp.concatenate([kw, q_eff], axis=0)
            od_ref[rs, ls] = aw_au[:, PAIR_W:]
    for c, (rs, gl) in enumerate(zip(rows, g_last)):
        yield MXU
        state = [st_ref[p] for p in range(PAIRS)]
        prod = [_dot(kwq_ref[c, p], st.astype(BF16)) for p, st in enumerate(state)]
        for p, ls in enumerate(lanes):
            st_ref[p] = state[p] * jnp.exp(gl[:, ls]) - prod[p][0:PAIR_W] + ku_ref[c, p]
            od_ref[rs, ls] = od_ref[rs, ls] + prod[p][PAIR_W:]
    yield MXU
    o_d = od_ref[...]
    ms = _seg_mean(o_d * o_d, seg)
    yield VPU
    return (o_d * lax.rsqrt(ms + NORM_EPS) * gnorm_ref[...] * _silu(z)).astype(BF16)


def _odd_kernel(x_ref, gain_ref,
                wsu_ref, bre_ref, bim_ref, pre_ref, pim_ref, cre_ref, cim_ref, dskip_ref, wglu_ref, bglu_ref,
                wqkvz_ref, wda_ref, wdb_ref, cw_ref, alog_ref, dtb_ref, gnorm_ref, seg_ref, tril_ref,
                onesbd_ref, woc_ref, wod_ref,
                o_ref,
                buf_ref, sst_ref, xs_ref, gst_ref, cs_ref, od_ref, attn_ref, uw_ref, ku_ref, kwq_ref):
    tile = x_ref.shape[1]

    @pl.when(pl.program_id(1) == 0)
    def _():
        sst_ref[...] = jnp.zeros_like(sst_ref)
        buf_ref[:, :, :, 0:SUBLANES, :] = jnp.zeros((3, 2, S5_LANE_BLOCKS, SUBLANES, LANES), F32)
        gst_ref[...] = jnp.zeros_like(gst_ref)
        cs_ref[:, 0:SUBLANES, :] = jnp.zeros((3 * MIX_W // LANES, SUBLANES, LANES), F32)

    x = x_ref[0]
    h = _rms_rows(x, gain_ref[...]).astype(BF16)
    o_c, o_d = _interleave(
        [_s5_steps(h, tile, wsu_ref, bre_ref, bim_ref, pre_ref, pim_ref, cre_ref, cim_ref,
                   dskip_ref, wglu_ref, bglu_ref, buf_ref, sst_ref, xs_ref),
         _gdn_steps(h, tile, wqkvz_ref, wda_ref, wdb_ref, cw_ref, alog_ref, dtb_ref, gnorm_ref, seg_ref,
                    tril_ref, onesbd_ref, gst_ref, cs_ref, od_ref, attn_ref, uw_ref, ku_ref, kwq_ref)],
        ODD_WEIGHTS)
    o_ref[0] = x + _dot(o_c, woc_ref[...]) + _dot(o_d, wod_ref[...])


def _odd_layer(x, gain, w_in, a_re, a_im, log_dt, b_re, b_im, c_re, c_im, d_skip, w_glu, b_glu,
               conv_w, a_log, dt_bias, gdn_norm, w_out):
    bsz, seq, _ = x.shape
    tile = min(MIX_TILE, seq)
    su1 = MIX_W
    z1 = su1 + 4 * MIX_W
    gs = (S5_GROUPS, S5_STATE)
    pow_re, pow_im, bb_re, bb_im = pl.pallas_call(
        _s5_prep_kernel,
        out_shape=[jax.ShapeDtypeStruct((SCAN_LEVELS,) + gs, F32)] * 2
        + [jax.ShapeDtypeStruct((S5_GROUP,) + gs, F32)] * 2,
        name="s5_discretize",
    )(a_re, a_im, log_dt.reshape(S5_GROUPS, 1), b_re.transpose(2, 0, 1), b_im.transpose(2, 0, 1))
    half_g = S5_GROUPS // 2
    eye = jnp.eye(half_g, dtype=F32)

    def in_blocks(bb):
        t = bb.reshape(S5_GROUP, 2, half_g, S5_STATE)
        return jnp.einsum('ckgp,gh->kgchp', t, eye).reshape(2, half_g * S5_GROUP, half_g * S5_STATE).astype(BF16)

    def out_blocks(cc):
        t = cc.reshape(2, half_g, S5_GROUP, S5_STATE)
        return jnp.einsum('kgop,gh->kgpho', t, eye).reshape(2, half_g * S5_STATE, half_g * S5_GROUP).astype(BF16)

    def lane_blocks(pw):
        return pw.reshape(SCAN_LEVELS, S5_LANE_BLOCKS, LANES).transpose(1, 0, 2)

    rep = lambda w: jnp.repeat(w, HEAD_DIM, axis=-1)
    args = (x, gain.reshape(1, D_MODEL),
            w_in[:, :su1].astype(BF16), in_blocks(bb_re), in_blocks(bb_im),
            lane_blocks(pow_re), lane_blocks(pow_im), out_blocks(c_re), out_blocks(c_im),
            d_skip.reshape(1, MIX_W), w_glu.astype(BF16), b_glu.reshape(1, MIX_W),
            w_in[:, su1:z1].astype(BF16), rep(w_in[:, z1:z1 + HEADS]).astype(BF16),
            rep(w_in[:, z1 + HEADS:]).astype(BF16), conv_w, rep(a_log).reshape(1, MIX_W),
            rep(dt_bias).reshape(1, MIX_W), _head_tile(gdn_norm, HEADS), _seg_matrix(),
            _block_tril(tile), _block_ones(tile), w_out[:MIX_W].astype(BF16), w_out[MIX_W:].astype(BF16))
    in_specs = [_tile_spec(tile, D_MODEL)] + [_const_spec(a.shape) for a in args[1:]]
    return pl.pallas_call(
        _odd_kernel,
        grid=(bsz, seq // tile),
        in_specs=in_specs,
        out_specs=_tile_spec(tile, D_MODEL),
        out_shape=jax.ShapeDtypeStruct(x.shape, F32),
        scratch_shapes=[pltpu.VMEM((3, 2, S5_LANE_BLOCKS, SUBLANES + tile, LANES), F32),
                        pltpu.VMEM((2, S5_LANE_BLOCKS, SUBLANES, LANES), F32),
                        pltpu.VMEM((2, S5_LANE_BLOCKS, tile, LANES), BF16),
                        pltpu.VMEM((PAIRS, PAIR_W, PAIR_W), F32),
                        pltpu.VMEM((3 * MIX_W // LANES, SUBLANES + tile, LANES), F32),
                        pltpu.VMEM((tile, MIX_W), F32),
                        pltpu.VMEM((tile, MIX_W), BF16),
                        pltpu.VMEM((tile // CHUNK, PAIRS, CHUNK, 2 * PAIR_W), BF16),
                        pltpu.VMEM((tile // CHUNK, PAIRS, PAIR_W, PAIR_W), F32),
                        pltpu.VMEM((tile // CHUNK, PAIRS, PAIR_W + CHUNK, PAIR_W), BF16)],
        compiler_params=_params(),
        name="s5_gdn_mixer",
    )(*args)


def kernel(x, norm_mix, norm_ffn, w_in_even, w_gla_gate, b_gla_gate, gla_out_norm, swa_q_norm, swa_k_norm, swa_sinks, w_out_even, w_in_odd, s5_a_re, s5_a_im, s5_log_dt, s5_b_re, s5_b_im, s5_c_re, s5_c_im, s5_d, s5_w_glu, s5_b_glu, gdn_conv_w, gdn_a_log, gdn_dt_bias, gdn_out_norm, w_out_odd, w_ffn_up, ffn_conv_w, ffn_conv_b, w_ffn_down):
    depth = norm_mix.shape[0]
    ffn_gains = norm_ffn.reshape(depth, 1, D_MODEL)
    ffn_up = w_ffn_up.astype(BF16)
    ffn_down = w_ffn_down.astype(BF16)
    ffn_bias = ffn_conv_b.reshape(depth, 1, 2 * FFN_DIM)
    for layer in range(depth):
        i = layer // 2
        if layer % 2 == 0:
            x = _even_layer(x, norm_mix[layer], w_in_even[i], w_gla_gate[i], b_gla_gate[i],
                            gla_out_norm[i], swa_q_norm[i], swa_k_norm[i], swa_sinks[i], w_out_even[i])
        else:
            x = _odd_layer(x, norm_mix[layer], w_in_odd[i], s5_a_re[i], s5_a_im[i], s5_log_dt[i],
                           s5_b_re[i], s5_b_im[i], s5_c_re[i], s5_c_im[i], s5_d[i], s5_w_glu[i],
                           s5_b_glu[i], gdn_conv_w[i], gdn_a_log[i], gdn_dt_bias[i], gdn_out_norm[i],
                           w_out_odd[i])
        x = _ffn_layer(x, layer, ffn_gains, ffn_up, ffn_conv_w, ffn_bias, ffn_down)
    return x
```
